```python
import math, functools
import jax, jax.numpy as jnp
from jax import lax
import numpy as np

D_MODEL = 2048
BATCH = 8
SEQ = 2048
DEPTH = 1
DEC_BATCH = 32
DEC_SEQ = 1
PAST_LEN = 16384
PAGE_SIZE = 128

HEAD_DIM = 128
GDN_HEADS = 8
SB_HEADS = 8
GDN_WIDTH = GDN_HEADS * HEAD_DIM
SB_WIDTH = SB_HEADS * HEAD_DIM
MIX_WIDTH = GDN_WIDTH + SB_WIDTH
CONV_W = 4
GDN_CONV_DIM = 3 * GDN_WIDTH
GDN_CHUNK = 64
SB_QBLOCK = 128
SB_BIAS_INIT = -8.0
D_FF = ((8 * D_MODEL + 3 * 256 - 1) // (3 * 256)) * 256
SPLIT_Z = GDN_CONV_DIM
SPLIT_B = SPLIT_Z + GDN_WIDTH
SPLIT_A = SPLIT_B + GDN_HEADS
SPLIT_SB = SPLIT_A + GDN_HEADS
IN_DIM = SPLIT_SB + 3 * SB_WIDTH
NORM_EPS = 1e-6

kernel_name = "hybrid_gdn_stickbreak_decoder_step"


def rmsnorm(x, w):
    xf = x.astype(jnp.float32)
    y = xf * lax.rsqrt(jnp.mean(xf * xf, axis=-1, keepdims=True) + NORM_EPS)
    return (y * w.astype(jnp.float32)).astype(x.dtype)


def l2norm(x):
    xf = x.astype(jnp.float32)
    return xf * lax.rsqrt(jnp.sum(xf * xf, axis=-1, keepdims=True) + NORM_EPS)


def causal_conv(x_ext, w):
    c = x_ext.shape[-1]
    return lax.conv_general_dilated(x_ext, w[:, None, :], window_strides=(1,), padding='VALID',
                                    dimension_numbers=('NWC', 'WIO', 'NWC'), feature_group_count=c)


def gated_delta_chunked(q, k, v, g, beta, state0, chunk):
    B, T, H, DK = q.shape
    nc = T // chunk
    f32 = jnp.float32

    def to_chunks(t):
        t = t.astype(f32).reshape((B, nc, chunk, H) + t.shape[3:])
        return jnp.moveaxis(t, 3, 1)

    q = to_chunks(q) * (DK ** -0.5)
    k = to_chunks(k)
    v = to_chunks(v)
    g = to_chunks(g)
    beta = to_chunks(beta)
    gc = jnp.cumsum(g, axis=-1)
    idx = jnp.arange(chunk)
    incl = idx[:, None] >= idx[None, :]
    strict = idx[:, None] > idx[None, :]
    decay = jnp.exp(jnp.where(incl, gc[..., :, None] - gc[..., None, :], -jnp.inf))
    kb = k * beta[..., None]
    lower = jnp.where(strict, jnp.einsum('bhnid,bhnjd->bhnij', kb, k) * decay, 0.0)
    eye = jnp.eye(chunk, dtype=f32)
    tinv = lax.linalg.triangular_solve(eye + lower, jnp.broadcast_to(eye, lower.shape),
                                       left_side=True, lower=True, unit_diagonal=True)
    u = tinv @ (v * beta[..., None])
    w = tinv @ (kb * jnp.exp(gc)[..., None])
    qk = jnp.where(incl, jnp.einsum('bhnid,bhnjd->bhnij', q, k) * decay, 0.0)
    g_last = gc[..., -1]
    k_tail = k * jnp.exp(g_last[..., None] - gc)[..., None]
    q_head = q * jnp.exp(gc)[..., None]
    xs = tuple(jnp.moveaxis(t, 2, 0) for t in (u, w, qk, k_tail, q_head, g_last))

    def step(S, inp):
        u_c, w_c, qk_c, kt_c, qh_c, gl_c = inp
        v_new = u_c - jnp.einsum('bhck,bhkv->bhcv', w_c, S)
        o_c = jnp.einsum('bhck,bhkv->bhcv', qh_c, S) + jnp.einsum('bhij,bhjv->bhiv', qk_c, v_new)
        S = S * jnp.exp(gl_c)[..., None, None] + jnp.einsum('bhck,bhcv->bhkv', kt_c, v_new)
        return S, o_c

    S, o = lax.scan(step, state0.astype(f32), xs)
    o = jnp.moveaxis(jnp.moveaxis(o, 0, 2), 1, 3).reshape(B, T, H, v.shape[-1])
    return o, S


def gdn_branch(qkv_raw, z, b_logit, a_logit, conv_buf, state0, conv_w, a_log, dt_bias, norm_w):
    B, T, _ = qkv_raw.shape
    ext = jnp.concatenate([conv_buf.astype(qkv_raw.dtype), qkv_raw], axis=1)
    new_buf = ext[:, -(CONV_W - 1):]
    qkv = jax.nn.silu(causal_conv(ext, conv_w.astype(qkv_raw.dtype)))
    q, k, v = [t.reshape(B, T, GDN_HEADS, HEAD_DIM) for t in jnp.split(qkv, 3, axis=-1)]
    beta = jax.nn.sigmoid(b_logit.astype(jnp.float32))
    g = -jnp.exp(a_log.astype(jnp.float32)) * jax.nn.softplus(
        a_logit.astype(jnp.float32) + dt_bias.astype(jnp.float32))
    o, S = gated_delta_chunked(l2norm(q), l2norm(k), v, g, beta, state0, math.gcd(T, GDN_CHUNK))
    zf = z.astype(jnp.float32).reshape(B, T, GDN_HEADS, HEAD_DIM)
    o = rmsnorm(o, norm_w) * jax.nn.silu(zf)
    return o.reshape(B, T, GDN_WIDTH).astype(qkv_raw.dtype), new_buf, S


def sb_block(q, k, v, q_pos, k_pos, carry, bias):
    z = jnp.einsum('bqhd,bkhd->bhqk', q, k, preferred_element_type=jnp.float32) * (HEAD_DIM ** -0.5)
    z = z + bias.astype(jnp.float32)[None, :, None, None]
    mask = k_pos[None, :] < q_pos[:, None]
    log_keep = jnp.where(mask, jax.nn.log_sigmoid(-z), 0.0)
    later = lax.cumsum(log_keep, axis=3, reverse=True) - log_keep
    weight = jnp.where(mask, jnp.exp(jax.nn.log_sigmoid(z) + later + carry[..., None]), 0.0)
    out = jnp.einsum('bhqk,bkhd->bqhd', weight, v.astype(jnp.float32))
    return out, carry + jnp.sum(log_keep, axis=-1)


def sb_prompt(q, k, v, bias):
    B, T, H, D = q.shape
    nb = T // SB_QBLOCK
    pos = jnp.arange(T)
    qb = jnp.moveaxis(q.reshape(B, nb, SB_QBLOCK, H, D), 1, 0)
    pb = pos.reshape(nb, SB_QBLOCK)

    def one_block(args):
        qq, pp = args
        o, _ = sb_block(qq, k, v, pp, pos, jnp.zeros((B, H, SB_QBLOCK), jnp.float32), bias)
        return o

    o = lax.map(one_block, (qb, pb))
    return jnp.moveaxis(o, 0, 1).reshape(B, T, H * D)


def sb_paged(q, k_new, v_new, bias, cache_k, cache_v, layer, page_table):
    B, Tq, H, D = q.shape
    n_pages = page_table.shape[1]
    page = cache_k.shape[2]
    qpos = n_pages * page + jnp.arange(Tq)
    acc, logs = sb_block(q, k_new, v_new, qpos, qpos, jnp.zeros((B, H, Tq), jnp.float32), bias)

    def step(carry, inp):
        acc_c, logs_c = carry
        phys, base = inp
        o, logs_c = sb_block(q, cache_k[layer, phys], cache_v[layer, phys], qpos,
                             base + jnp.arange(page), logs_c, bias)
        return (acc_c + o, logs_c), None

    xs = (page_table.T[::-1], (jnp.arange(n_pages) * page)[::-1])
    (acc, _), _ = lax.scan(step, (acc, logs), xs)
    return acc.reshape(B, Tq, H * D)


def trunk_layer(x, c, conv_buf, gdn_state, sb_mix, norm1_w, norm2_w, w_ada, b_ada, w_in, conv_w,
                a_log, dt_bias, gdn_norm_w, sb_bias, w_out, w_gate, w_up, w_down):
    B, T, _ = x.shape
    mod = (jax.nn.silu(c) @ w_ada + b_ada)[:, None, :]
    sh1, sc1, ga1, sh2, sc2, ga2 = jnp.split(mod, 6, axis=-1)
    h = rmsnorm(x, norm1_w) * (1 + sc1) + sh1
    proj = h @ w_in
    qkv_raw, z, b_logit, a_logit, sb_qkv = jnp.split(proj, [SPLIT_Z, SPLIT_B, SPLIT_A, SPLIT_SB], axis=-1)
    gdn_out, new_buf, new_state = gdn_branch(qkv_raw, z, b_logit, a_logit, conv_buf, gdn_state,
                                             conv_w, a_log, dt_bias, gdn_norm_w)
    sq, sk, sv = [t.reshape(B, T, SB_HEADS, HEAD_DIM) for t in jnp.split(sb_qkv, 3, axis=-1)]
    sb_out = sb_mix(sq, sk, sv, sb_bias).astype(x.dtype)
    x = x + ga1 * (jnp.concatenate([gdn_out, sb_out], axis=-1) @ w_out)
    h2 = rmsnorm(x, norm2_w) * (1 + sc2) + sh2
    x = x + ga2 * ((jax.nn.silu(h2 @ w_gate) * (h2 @ w_up)) @ w_down)
    return x, new_buf, new_state, sk, sv


def setup_inputs(seed: int = 0) -> dict:
    key = jax.random.key(seed)
    ks = jax.random.split(key, 24)
    f32 = jnp.float32
    n_pages = PAST_LEN // PAGE_SIZE
    n_pool = (5 * DEC_BATCH * n_pages + 3) // 4

    def nrm(k, shape, scale):
        return jax.random.normal(k, shape, f32) * scale

    page_table = jax.random.permutation(ks[8], n_pool)[: DEC_BATCH * n_pages].reshape(
        DEC_BATCH, n_pages).astype(jnp.int32)
    dt = jnp.exp(jax.random.uniform(ks[15], (DEPTH, GDN_HEADS), f32, math.log(1e-3), math.log(1e-1)))
    return {
        "x_prompt": nrm(ks[0], (BATCH, SEQ, D_MODEL), 1.0),
        "x_sample": nrm(ks[1], (DEC_BATCH, DEC_SEQ, D_MODEL), 1.0),
        "c_prompt": nrm(ks[2], (BATCH, D_MODEL), 1.0),
        "c_sample": nrm(ks[3], (DEC_BATCH, D_MODEL), 1.0),
        "cache_k": nrm(ks[4], (DEPTH, n_pool, PAGE_SIZE, SB_HEADS, HEAD_DIM), 1.0),
        "cache_v": nrm(ks[5], (DEPTH, n_pool, PAGE_SIZE, SB_HEADS, HEAD_DIM), 1.0),
        "state_gdn": nrm(ks[6], (DEPTH, DEC_BATCH, GDN_HEADS, HEAD_DIM, HEAD_DIM), 0.1),
        "state_conv": nrm(ks[7], (DEPTH, DEC_BATCH, CONV_W - 1, GDN_CONV_DIM), 1.0),
        "page_table": page_table,
        "norm1_w": 1.0 + nrm(ks[9], (DEPTH, D_MODEL), 0.02),
        "norm2_w": 1.0 + nrm(ks[10], (DEPTH, D_MODEL), 0.02),
        "w_ada": nrm(ks[11], (DEPTH, D_MODEL, 6 * D_MODEL), 0.5 * D_MODEL ** -0.5),
        "b_ada": nrm(ks[12], (DEPTH, 6 * D_MODEL), 0.02),
        "w_in": nrm(ks[13], (DEPTH, D_MODEL, IN_DIM), D_MODEL ** -0.5),
        "conv_w": nrm(ks[14], (DEPTH, CONV_W, GDN_CONV_DIM), CONV_W ** -0.5),
        "a_log": jnp.log(jax.random.uniform(ks[16], (DEPTH, GDN_HEADS), f32, 1.0, 16.0)),
        "dt_bias": dt + jnp.log(-jnp.expm1(-dt)),
        "gdn_norm_w": 1.0 + nrm(ks[17], (DEPTH, HEAD_DIM), 0.02),
        "sb_bias": SB_BIAS_INIT + nrm(ks[23], (DEPTH, SB_HEADS), 0.1),
        "w_out": nrm(ks[18], (DEPTH, MIX_WIDTH, D_MODEL), MIX_WIDTH ** -0.5),
        "w_gate": nrm(ks[19], (DEPTH, D_MODEL, D_FF), D_MODEL ** -0.5),
        "w_up": nrm(ks[20], (DEPTH, D_MODEL, D_FF), D_MODEL ** -0.5),
        "w_down": nrm(ks[21], (DEPTH, D_FF, D_MODEL), D_FF ** -0.5),
        "final_norm_w": 1.0 + nrm(ks[22], (D_MODEL,), 0.02),
    }


def reference(x_prompt, x_sample, c_prompt, c_sample, cache_k, cache_v, state_gdn, state_conv, page_table,
              norm1_w, norm2_w, w_ada, b_ada, w_in, conv_w, a_log, dt_bias, gdn_norm_w, sb_bias, w_out,
              w_gate, w_up, w_down, final_norm_w):
    hp, hs = x_prompt, x_sample
    kp_rows, vp_rows, gdn_p, conv_p = [], [], [], []
    ks_rows, vs_rows, gdn_s, conv_s = [], [], [], []
    for l in range(DEPTH):
        layer_w = (norm1_w[l], norm2_w[l], w_ada[l], b_ada[l], w_in[l], conv_w[l], a_log[l], dt_bias[l],
                   gdn_norm_w[l], sb_bias[l], w_out[l], w_gate[l], w_up[l], w_down[l])
        zero_buf = jnp.zeros((hp.shape[0], CONV_W - 1, GDN_CONV_DIM), hp.dtype)
        zero_state = jnp.zeros((hp.shape[0], GDN_HEADS, HEAD_DIM, HEAD_DIM), jnp.float32)
        hp, buf, st, kk, vv = trunk_layer(hp, c_prompt, zero_buf, zero_state, sb_prompt, *layer_w)
        kp_rows.append(kk)
        vp_rows.append(vv)
        gdn_p.append(st.astype(hp.dtype))
        conv_p.append(buf)
        sb_mix = functools.partial(sb_paged, cache_k=cache_k, cache_v=cache_v, layer=l, page_table=page_table)
        hs, buf, st, kk, vv = trunk_layer(hs, c_sample, state_conv[l], state_gdn[l], sb_mix, *layer_w)
        ks_rows.append(kk)
        vs_rows.append(vv)
        gdn_s.append(st.astype(state_gdn.dtype))
        conv_s.append(buf.astype(state_conv.dtype))
    y_prompt = rmsnorm(hp, final_norm_w)
    y_sample = rmsnorm(hs, final_norm_w)
    return (y_prompt, y_sample, jnp.stack(kp_rows), jnp.stack(vp_rows), jnp.stack(gdn_p), jnp.stack(conv_p),
            jnp.stack(ks_rows), jnp.stack(vs_rows), jnp.stack(gdn_s), jnp.stack(conv_s))
```

```python
import functools
import math

import jax
import jax.numpy as jnp
from jax import lax
from jax.experimental import pallas as pl
from jax.experimental.pallas import tpu as pltpu

F32 = jnp.float32
BF16 = jnp.bfloat16
NORM_EPS = 1e-6
LANES = 128
SUBLANES = 8
GDN_CHUNK = 128
INV_BLOCK = 16
SB_BLOCK = 256
PAGES_PER_STEP = 4
VMEM_LIMIT = 56 * 1024 * 1024


def _cparams(sem):
    return pltpu.CompilerParams(dimension_semantics=sem, vmem_limit_bytes=VMEM_LIMIT)


def _tile(n, pref):
    if n <= pref:
        return n
    t = pref
    while n % t:
        t //= 2
    return t


def _mm(a, b):
    return jnp.dot(a.astype(BF16), b.astype(BF16), preferred_element_type=F32)


def _mm_nt(a, b):
    return lax.dot_general(a.astype(BF16), b.astype(BF16), (((1,), (1,)), ((), ())), preferred_element_type=F32)


def _mm_tn(a, b):
    return lax.dot_general(a.astype(BF16), b.astype(BF16), (((0,), (0,)), ((), ())), preferred_element_type=F32)


def _split3(x):
    hi = x.astype(BF16)
    r = x - hi.astype(F32)
    mid = r.astype(BF16)
    lo = (r - mid.astype(F32)).astype(BF16)
    return hi, mid, lo


def _dot_01_rhs(x, m01):
    return sum(jnp.dot(p, m01, preferred_element_type=F32) for p in _split3(x))


def _dot_01_lhs(m01, x):
    return sum(jnp.dot(m01, p, preferred_element_type=F32) for p in _split3(x))


def _softplus(x):
    return jnp.maximum(x, 0.0) + jnp.log1p(jnp.exp(-jnp.abs(x)))


def _silu(x):
    return x * jax.nn.sigmoid(x)


def _rms(x):
    return x * lax.rsqrt(jnp.mean(x * x, axis=-1, keepdims=True) + NORM_EPS)


def _ada_kernel(cp_ref, cs_ref, w_ref, b_ref, op_ref, os_ref):
    w = w_ref[...].astype(BF16)
    b = b_ref[...]
    for c_ref, o_ref in ((cp_ref, op_ref), (cs_ref, os_ref)):
        o_ref[...] = jnp.dot(_silu(c_ref[...]).astype(BF16), w, preferred_element_type=F32) + b


def _ada(c_p, c_s, w, b):
    d, n = w.shape
    tn = _tile(n, 1024)
    bp, bs = c_p.shape[0], c_s.shape[0]
    return pl.pallas_call(
        _ada_kernel,
        grid=(n // tn,),
        in_specs=[
            pl.BlockSpec((bp, d), lambda j: (0, 0)),
            pl.BlockSpec((bs, d), lambda j: (0, 0)),
            pl.BlockSpec((d, tn), lambda j: (0, j)),
            pl.BlockSpec((1, tn), lambda j: (0, j)),
        ],
        out_specs=[pl.BlockSpec((bp, tn), lambda j: (0, j)), pl.BlockSpec((bs, tn), lambda j: (0, j))],
        out_shape=[jax.ShapeDtypeStruct((bp, n), F32), jax.ShapeDtypeStruct((bs, n), F32)],
        compiler_params=_cparams(("parallel",)),
        name="ada_mod",
    )(c_p, c_s, w, b.reshape(1, n))


def _mod_spec(mod3, comp, d, tm, rows_per_group, tn=None):
    per_row = mod3.shape[1] != 1
    assert not per_row or mod3.shape[1] == tm
    if tn is None:
        col = lambda j: comp
        tn = d
    else:
        col = lambda j: comp * (d // tn) + j
    if per_row:
        return pl.BlockSpec((1, tm, tn), lambda i, j: (0, 0, col(j)))
    return pl.BlockSpec((1, 1, tn), lambda i, j: (i * tm // rows_per_group, 0, col(j)))


def _inproj_kernel(x_ref, nw_ref, sc_ref, sh_ref, w_ref, wba_ref, og_ref, oq_ref, ok_ref, ov_ref, oba_ref, *rest,
                   n_g, n_h, n_bat):
    obat_ref = rest[0] if n_bat else None
    h_scr = rest[-1]
    j = pl.program_id(1)

    @pl.when(j == 0)
    def _():
        h = _rms(x_ref[...]) * nw_ref[...]
        h = (h * (1.0 + sc_ref[0]) + sh_ref[0]).astype(BF16)
        h_scr[...] = h
        ba = jnp.dot(h, wba_ref[...], preferred_element_type=F32)
        oba_ref[...] = ba
        if n_bat:
            obat_ref[...] = ba.T[:n_bat]

    acc = jnp.dot(h_scr[...], w_ref[...], preferred_element_type=F32)
    for ref, lo, hi in ((og_ref, 0, n_g), (oq_ref, n_g, n_g + n_h), (ok_ref, n_g + n_h, n_g + 2 * n_h),
                        (ov_ref, n_g + 2 * n_h, n_g + 3 * n_h)):
        @pl.when((j >= lo) & (j < hi))
        def _(ref=ref):
            ref[...] = acc


def _inproj(x, nw, mod3, rows_per_group, w_main, w_ba, gdn_cols, sb_cols, n_bat):
    m, d = x.shape
    tm = _tile(m, 512)
    tn = _tile(math.gcd(gdn_cols, sb_cols), 512)
    n_g, n_h = gdn_cols // tn, sb_cols // tn
    nj = n_g + 3 * n_h

    def seg(lo, n):
        return lambda i, j: (i, jnp.clip(j - lo, 0, n - 1))

    kern = functools.partial(_inproj_kernel, n_g=n_g, n_h=n_h, n_bat=n_bat)
    return pl.pallas_call(
        kern,
        grid=(m // tm, nj),
        in_specs=[
            pl.BlockSpec((tm, d), lambda i, j: (i, 0)),
            pl.BlockSpec((1, d), lambda i, j: (0, 0)),
            _mod_spec(mod3, 1, d, tm, rows_per_group),
            _mod_spec(mod3, 0, d, tm, rows_per_group),
            pl.BlockSpec((d, tn), lambda i, j: (0, j)),
            pl.BlockSpec((d, LANES), lambda i, j: (0, 0)),
        ],
        out_specs=[
            pl.BlockSpec((tm, tn), seg(0, n_g)),
            pl.BlockSpec((tm, tn), seg(n_g, n_h)),
            pl.BlockSpec((tm, tn), seg(n_g + n_h, n_h)),
            pl.BlockSpec((tm, tn), seg(n_g + 2 * n_h, n_h)),
            pl.BlockSpec((tm, LANES), lambda i, j: (i, 0)),
        ] + ([pl.BlockSpec((n_bat, tm), lambda i, j: (0, i))] if n_bat else []),
        out_shape=[
            jax.ShapeDtypeStruct((m, gdn_cols), F32),
            jax.ShapeDtypeStruct((m, sb_cols), F32),
            jax.ShapeDtypeStruct((m, sb_cols), F32),
            jax.ShapeDtypeStruct((m, sb_cols), F32),
            jax.ShapeDtypeStruct((m, LANES), F32),
        ] + ([jax.ShapeDtypeStruct((n_bat, m), F32)] if n_bat else []),
        scratch_shapes=[pltpu.VMEM((tm, d), BF16)],
        compiler_params=_cparams(("parallel", "arbitrary")),
        name="in_proj",
    )(x, nw.reshape(1, d), mod3, mod3, w_main, w_ba)


def _unit_lower_inverse_minus_eye(a, ri, ci):
    c = a.shape[0]
    blk = (ri // INV_BLOCK) == (ci // INV_BLOCK)
    ad = jnp.where(blk, a, 0.0)
    off = a - ad
    xn = -ad
    p = ad
    for _ in range(int(math.log2(INV_BLOCK)) - 1):
        p = _mm(p, p)
        xn = xn + p + _mm(xn, p)
    m = off + _mm(xn, off)
    yn = -m
    p = m
    for _ in range(int(math.log2(c // INV_BLOCK)) - 1):
        p = _mm(p, p)
        yn = yn + p + _mm(yn, p)
    return yn + xn + _mm(yn, xn)


def _gdn_chunk_kernel(g_ref, ba_ref, bat_ref, cb_ref, s0_ref, cw_ref, arow_ref, drow_ref, acol_ref, dcol_ref, nw_ref,
                      o_ref, s_ref, hist_ref, *, nh, hd, chunk, kw):
    c = pl.program_id(1)

    @pl.when(c == 0)
    def _():
        s_ref[...] = s0_ref[...]
        hist_ref[...] = cb_ref[...]

    gw = nh * hd
    ri = lax.broadcasted_iota(jnp.int32, (chunk, chunk), 0)
    ci = lax.broadcasted_iota(jnp.int32, (chunk, chunk), 1)
    incl = ci <= ri
    strict = ci < ri
    ltri = incl.astype(BF16)
    utri = (ri <= ci).astype(BF16)

    ba = ba_ref[...]
    bat = bat_ref[...]
    beta_all = jax.nn.sigmoid(ba)
    g_all = -jnp.exp(arow_ref[...]) * _softplus(ba + drow_ref[...])
    g_t = -jnp.exp(acol_ref[...]) * _softplus(bat + dcol_ref[...])
    gc_all = _dot_01_lhs(ltri, g_all)
    gc_t = _dot_01_rhs(g_t, utri)
    scale = hd ** -0.5

    hist = hist_ref[0]
    x_all = g_ref[...]
    cw = cw_ref[...]
    nw = nw_ref[...]
    pad = hist.shape[0]

    for h in range(nh):
        def conv(s):
            lo = s * gw + h * hd
            xw = jnp.concatenate([hist[:, lo:lo + hd], x_all[:, lo:lo + hd]], axis=0)
            y = cw[kw - 1:kw, lo:lo + hd] * xw[pad:pad + chunk]
            for i in range(kw - 1):
                sft = kw - 1 - i
                y = y + cw[i:i + 1, lo:lo + hd] * xw[pad - sft:pad - sft + chunk]
            return _silu(y)

        def l2n(t):
            return t * lax.rsqrt(jnp.sum(t * t, axis=-1, keepdims=True) + NORM_EPS)

        q = l2n(conv(0)) * scale
        k = l2n(conv(1))
        v = conv(2)
        beta = beta_all[:, h:h + 1]
        gc = gc_all[:, nh + h:nh + h + 1]
        gc_row = gc_t[nh + h:nh + h + 1, :]
        gl = gc_all[chunk - 1:chunk, nh + h:nh + h + 1]
        decay = jnp.exp(jnp.where(incl, gc - gc_row, -1e30))
        kb = k * beta
        a = jnp.where(strict, _mm_nt(kb, k) * decay, 0.0)
        tn = _unit_lower_inverse_minus_eye(a, ri, ci)
        egc = jnp.exp(gc)
        r = jnp.concatenate([v * beta, kb * egc], axis=1)
        uw = r + _mm(tn, r)
        u, w = uw[:, :hd], uw[:, hd:]
        qk = jnp.where(incl, _mm_nt(q, k) * decay, 0.0)
        s = s_ref[0, h]
        v_new = u - _mm(w, s)
        o = _mm(q * egc, s) + _mm(qk, v_new)
        s_ref[0, h] = s * jnp.exp(gl) + _mm_tn(k * jnp.exp(gl - gc), v_new)
        z = x_all[:, 3 * gw + h * hd:3 * gw + (h + 1) * hd]
        o_ref[:, h * hd:(h + 1) * hd] = (_rms(o) * nw * _silu(z)).astype(o_ref.dtype)

    hist_ref[0] = x_all[chunk - pad:, :3 * gw]


def _gdn_chunked(gdn, ba, bat, conv_buf, state0, conv_w, a_log, dt_bias, norm_w, batch, seq):
    m, gcols = gdn.shape
    nh = a_log.shape[0]
    hd = norm_w.shape[0]
    gw = nh * hd
    kw = conv_w.shape[0]
    chunk = _tile(seq, GDN_CHUNK)
    nc = seq // chunk
    n_bat = bat.shape[0]
    pad = SUBLANES
    cb = jnp.zeros((batch, pad, 3 * gw), F32).at[:, pad - (kw - 1):].set(conv_buf.astype(F32))
    cw = jnp.zeros((pad, 3 * gw), F32).at[:kw].set(conv_w.astype(F32))
    arow = jnp.zeros((1, LANES), F32).at[0, nh:2 * nh].set(a_log)
    drow = jnp.zeros((1, LANES), F32).at[0, nh:2 * nh].set(dt_bias)
    acol = jnp.zeros((n_bat, 1), F32).at[nh:2 * nh, 0].set(a_log)
    dcol = jnp.zeros((n_bat, 1), F32).at[nh:2 * nh, 0].set(dt_bias)
    kern = functools.partial(_gdn_chunk_kernel, nh=nh, hd=hd, chunk=chunk, kw=kw)
    const2 = lambda b, c: (0, 0)
    o, s, hist = pl.pallas_call(
        kern,
        grid=(batch, nc),
        in_specs=[
            pl.BlockSpec((chunk, gcols), lambda b, c: (b * nc + c, 0)),
            pl.BlockSpec((chunk, LANES), lambda b, c: (b * nc + c, 0)),
            pl.BlockSpec((n_bat, chunk), lambda b, c: (0, b * nc + c)),
            pl.BlockSpec((1, pad, 3 * gw), lambda b, c: (b, 0, 0)),
            pl.BlockSpec((1, nh, hd, hd), lambda b, c: (b, 0, 0, 0)),
            pl.BlockSpec((pad, 3 * gw), const2),
            pl.BlockSpec((1, LANES), const2),
            pl.BlockSpec((1, LANES), const2),
            pl.BlockSpec((n_bat, 1), const2),
            pl.BlockSpec((n_bat, 1), const2),
            pl.BlockSpec((1, hd), const2),
        ],
        out_specs=[
            pl.BlockSpec((chunk, gw), lambda b, c: (b * nc + c, 0)),
            pl.BlockSpec((1, nh, hd, hd), lambda b, c: (b, 0, 0, 0)),
            pl.BlockSpec((1, pad, 3 * gw), lambda b, c: (b, 0, 0)),
        ],
        out_shape=[
            jax.ShapeDtypeStruct((m, gw), BF16),
            jax.ShapeDtypeStruct((batch, nh, hd, hd), F32),
            jax.ShapeDtypeStruct((batch, pad, 3 * gw), F32),
        ],
        compiler_params=_cparams(("parallel", "arbitrary")),
        name="gdn_chunked",
    )(gdn, ba, bat, cb, state0.astype(F32), cw, arow, drow, acol, dcol, norm_w.reshape(1, hd).astype(F32))
    return o, s, hist[:, pad - (kw - 1):]


def _gdn_step_kernel(g_ref, ba_ref, cb_ref, s0_ref, cw_ref, arow_ref, drow_ref, nw_ref, o_ref, s_ref, *, nh, hd, kw):
    gw = nh * hd
    x = g_ref[0]
    cb = cb_ref[0]
    cw = cw_ref[...]
    y = cw[kw - 1:kw] * x[:, :3 * gw]
    for i in range(kw - 1):
        y = y + cw[i:i + 1] * cb[i:i + 1]
    y = _silu(y)
    ba = ba_ref[0]
    beta_all = jax.nn.sigmoid(ba)
    g_all = -jnp.exp(arow_ref[...]) * _softplus(ba + drow_ref[...])
    nw = nw_ref[...]
    row = lax.broadcasted_iota(jnp.int32, (SUBLANES, hd), 0)
    scale = hd ** -0.5

    def l2n(t):
        return t * lax.rsqrt(jnp.sum(t * t, axis=-1, keepdims=True) + NORM_EPS)

    for h in range(nh):
        q = l2n(y[:, h * hd:(h + 1) * hd]) * scale
        k = l2n(y[:, gw + h * hd:gw + (h + 1) * hd])
        v = y[:, 2 * gw + h * hd:2 * gw + (h + 1) * hd]
        beta = beta_all[:, h:h + 1]
        eg = jnp.exp(g_all[:, nh + h:nh + h + 1])
        s = s0_ref[0, h]
        lhs = jnp.where(row == 0, jnp.broadcast_to(k, (SUBLANES, hd)),
                        jnp.where(row == 1, jnp.broadcast_to(q, (SUBLANES, hd)), 0.0))
        ks_qs = _mm(lhs, s)
        ks, qs = ks_qs[0:1], ks_qs[1:2]
        v_new = beta * v - (beta * eg) * ks
        qk = jnp.sum(q * k, axis=-1, keepdims=True)
        o = eg * qs + qk * v_new
        k8 = jnp.where(row == 0, jnp.broadcast_to(k, (SUBLANES, hd)), 0.0)
        v8 = jnp.where(row == 0, jnp.broadcast_to(v_new, (SUBLANES, hd)), 0.0)
        s_ref[0, h] = s * eg + _mm_tn(k8, v8)
        z = x[:, 3 * gw + h * hd:3 * gw + (h + 1) * hd]
        o_ref[0, :, h * hd:(h + 1) * hd] = (_rms(o) * nw * _silu(z)).astype(o_ref.dtype)


def _gdn_step(gdn, ba, conv_buf, state0, conv_w, a_log, dt_bias, norm_w):
    b, gcols = gdn.shape
    nh = a_log.shape[0]
    hd = norm_w.shape[0]
    gw = nh * hd
    kw = conv_w.shape[0]
    arow = jnp.zeros((1, LANES), F32).at[0, nh:2 * nh].set(a_log)
    drow = jnp.zeros((1, LANES), F32).at[0, nh:2 * nh].set(dt_bias)
    kern = functools.partial(_gdn_step_kernel, nh=nh, hd=hd, kw=kw)
    const2 = lambda i: (0, 0)
    o, s = pl.pallas_call(
        kern,
        grid=(b,),
        in_specs=[
            pl.BlockSpec((1, 1, gcols), lambda i: (i, 0, 0)),
            pl.BlockSpec((1, 1, LANES), lambda i: (i, 0, 0)),
            pl.BlockSpec((1, kw - 1, 3 * gw), lambda i: (i, 0, 0)),
            pl.BlockSpec((1, nh, hd, hd), lambda i: (i, 0, 0, 0)),
            pl.BlockSpec((kw, 3 * gw), const2),
            pl.BlockSpec((1, LANES), const2),
            pl.BlockSpec((1, LANES), const2),
            pl.BlockSpec((1, hd), const2),
        ],
        out_specs=[
            pl.BlockSpec((1, 1, gw), lambda i: (i, 0, 0)),
            pl.BlockSpec((1, nh, hd, hd), lambda i: (i, 0, 0, 0)),
        ],
        out_shape=[jax.ShapeDtypeStruct((b, 1, gw), BF16), jax.ShapeDtypeStruct((b, nh, hd, hd), F32)],
        compiler_params=_cparams(("parallel",)),
        name="gdn_step",
    )(gdn.reshape(b, 1, gcols), ba.reshape(b, 1, LANES), conv_buf.astype(F32), state0.astype(F32),
      conv_w.astype(F32), arow, drow, norm_w.reshape(1, hd).astype(F32))
    return o.reshape(b, gw), s


def _sb_block_update(q, k, v, bias, carry, acc, mask, utri):
    z = _mm_nt(q, k) + bias
    nlk = _softplus(z)
    if mask is not None:
        nlk = jnp.where(mask, nlk, 0.0)
    hi = nlk.astype(BF16)
    lo = (nlk - hi.astype(F32)).astype(BF16)
    later = jnp.dot(hi, utri, preferred_element_type=F32) + jnp.dot(lo, utri, preferred_element_type=F32)
    w = jnp.exp(z - nlk - later - carry)
    if mask is not None:
        w = jnp.where(mask, w, 0.0)
    acc = acc + _mm(w, v)
    return carry + jnp.sum(nlk, axis=-1, keepdims=True), acc


def _sb_prompt_kernel(bias_ref, q_ref, k_ref, v_ref, o_ref, *, blk, hd):
    h = pl.program_id(1)
    qi = pl.program_id(2)
    bias = bias_ref[h]
    q = (q_ref[...] * (hd ** -0.5)).astype(BF16)
    ri = lax.broadcasted_iota(jnp.int32, (blk, blk), 0)
    ci = lax.broadcasted_iota(jnp.int32, (blk, blk), 1)
    utri = (ri > ci).astype(BF16)
    start = pl.multiple_of(qi * blk, blk)
    carry, acc = _sb_block_update(q, k_ref[pl.ds(start, blk), :], v_ref[pl.ds(start, blk), :], bias,
                                  jnp.zeros((blk, 1), F32), jnp.zeros((blk, hd), F32), ci < ri, utri)

    def body(it, ca):
        s0 = pl.multiple_of((qi - 1 - it) * blk, blk)
        return _sb_block_update(q, k_ref[pl.ds(s0, blk), :], v_ref[pl.ds(s0, blk), :], bias, ca[0], ca[1], None, utri)

    carry, acc = lax.fori_loop(0, qi, body, (carry, acc))
    o_ref[...] = acc.astype(o_ref.dtype)


def _sb_prompt(q, k, v, bias, batch, seq, hd):
    m, width = q.shape
    nh = width // hd
    blk = _tile(seq, SB_BLOCK)
    nq = seq // blk
    kern = functools.partial(_sb_prompt_kernel, blk=blk, hd=hd)
    return pl.pallas_call(
        kern,
        grid=(batch, nh, nq),
        in_specs=[
            pl.BlockSpec(memory_space=pltpu.SMEM),
            pl.BlockSpec((blk, hd), lambda b, h, i: (b * nq + i, h)),
            pl.BlockSpec((seq, hd), lambda b, h, i: (b, h)),
            pl.BlockSpec((seq, hd), lambda b, h, i: (b, h)),
        ],
        out_specs=pl.BlockSpec((blk, hd), lambda b, h, i: (b * nq + i, h)),
        out_shape=jax.ShapeDtypeStruct((m, width), BF16),
        compiler_params=_cparams(("parallel", "parallel", "arbitrary")),
        name="sb_prompt",
    )(bias.astype(F32), q, k, v)


def _sb_paged_kernel(pt_ref, q_ref, knew_ref, vnew_ref, brow_ref, bcol_ref, *refs, nh, hd, page, npp, n_pages):
    k_refs, v_refs = refs[:npp], refs[npp:2 * npp]
    o_ref, carry_ref = refs[2 * npp], refs[2 * npp + 1]
    s = pl.program_id(1)
    width = page * nh
    nt = width // LANES
    sub = lax.broadcasted_iota(jnp.int32, (nh, width), 0)
    lane = lax.broadcasted_iota(jnp.int32, (nh, width), 1)
    own = (lane % nh) == sub
    ri = lax.broadcasted_iota(jnp.int32, (LANES, 2 * LANES), 0)
    ci = lax.broadcasted_iota(jnp.int32, (LANES, 2 * LANES), 1)
    usum = ((ri > ci) | (ci >= LANES)).astype(BF16)
    q = (q_ref[0] * (hd ** -0.5)).astype(BF16)
    brow = brow_ref[...]

    @pl.when(s == 0)
    def _():
        z = jnp.sum(q_ref[0] * knew_ref[0], axis=-1, keepdims=True) * (hd ** -0.5) + bcol_ref[...]
        pos = jnp.full((nh, 1), n_pages * page, jnp.int32)
        visible = pos < pos
        nlk = jnp.where(visible, _softplus(z), 0.0)
        w = jnp.where(visible, jnp.exp(z - nlk), 0.0)
        o_ref[0] = w * vnew_ref[0]
        carry_ref[...] = jnp.broadcast_to(nlk, (nh, LANES))

    acc = o_ref[0]
    carry = carry_ref[...]
    for p in range(npp):
        kp = k_refs[p][...]
        vp = v_refs[p][...]
        z = _mm_nt(q, kp) + brow
        nlk = jnp.where(own, _softplus(z), 0.0)
        tiles = jnp.concatenate([nlk[:, t * LANES:(t + 1) * LANES] for t in range(nt)], axis=0)
        hi = tiles.astype(BF16)
        lo = (tiles - hi.astype(F32)).astype(BF16)
        su = jnp.dot(hi, usum, preferred_element_type=F32) + jnp.dot(lo, usum, preferred_element_type=F32)
        later = []
        run = carry
        for t in reversed(range(nt)):
            later.append(su[t * nh:(t + 1) * nh, :LANES] + run)
            run = run + su[t * nh:(t + 1) * nh, LANES:]
        later = jnp.concatenate(later[::-1], axis=1)
        w = jnp.where(own, jnp.exp(z - nlk - later), 0.0)
        acc = acc + _mm(w, vp)
        carry = run
    o_ref[0] = acc
    carry_ref[...] = carry


def _sb_paged(q, k_new, v_new, bias, cache_k, cache_v, layer, page_table):
    b, nh, hd = q.shape
    n_pool, page = cache_k.shape[1], cache_k.shape[2]
    n_pages = page_table.shape[1]
    npp = _tile(n_pages, PAGES_PER_STEP)
    width = page * nh
    ck = cache_k.reshape(cache_k.shape[0], n_pool, width, hd)
    cv = cache_v.reshape(cache_v.shape[0], n_pool, width, hd)
    brow = jnp.tile(bias.astype(F32), page).reshape(1, width)

    def page_spec(p):
        return pl.BlockSpec((None, None, width, hd),
                            lambda i, s, pt: (layer, pt[i * n_pages + n_pages - 1 - (s * npp + p)], 0, 0))

    kern = functools.partial(_sb_paged_kernel, nh=nh, hd=hd, page=page, npp=npp, n_pages=n_pages)
    head_spec = pl.BlockSpec((1, nh, hd), lambda i, s, pt: (i, 0, 0))
    grid_spec = pltpu.PrefetchScalarGridSpec(
        num_scalar_prefetch=1,
        grid=(b, n_pages // npp),
        in_specs=[head_spec, head_spec, head_spec, pl.BlockSpec((1, width), lambda i, s, pt: (0, 0)),
                  pl.BlockSpec((nh, 1), lambda i, s, pt: (0, 0))]
        + [page_spec(p) for p in range(npp)] + [page_spec(p) for p in range(npp)],
        out_specs=head_spec,
        scratch_shapes=[pltpu.VMEM((nh, LANES), F32)],
    )
    return pl.pallas_call(
        kern,
        grid_spec=grid_spec,
        out_shape=jax.ShapeDtypeStruct((b, nh, hd), F32),
        compiler_params=_cparams(("parallel", "arbitrary")),
        name="sb_paged",
    )(page_table.reshape(-1).astype(jnp.int32), q, k_new, v_new, brow, bias.astype(F32).reshape(nh, 1),
      *([ck] * npp), *([cv] * npp))


def _outproj_kernel(x_ref, ga_ref, a_ref, b_ref, wa_ref, wb_ref, o_ref):
    y = jnp.dot(a_ref[...], wa_ref[...], preferred_element_type=F32)
    y = y + jnp.dot(b_ref[...], wb_ref[...], preferred_element_type=F32)
    o_ref[...] = x_ref[...] + ga_ref[0] * y


def _outproj(x, mod3, rows_per_group, mix_a, mix_b, w_a, w_b):
    m, d = x.shape
    ka, kb = mix_a.shape[1], mix_b.shape[1]
    tm = _tile(m, 512)
    tn = _tile(d, 512)
    return pl.pallas_call(
        _outproj_kernel,
        grid=(m // tm, d // tn),
        in_specs=[
            pl.BlockSpec((tm, tn), lambda i, j: (i, j)),
            _mod_spec(mod3, 2, d, tm, rows_per_group, tn),
            pl.BlockSpec((tm, ka), lambda i, j: (i, 0)),
            pl.BlockSpec((tm, kb), lambda i, j: (i, 0)),
            pl.BlockSpec((ka, tn), lambda i, j: (0, j)),
            pl.BlockSpec((kb, tn), lambda i, j: (0, j)),
        ],
        out_specs=pl.BlockSpec((tm, tn), lambda i, j: (i, j)),
        out_shape=jax.ShapeDtypeStruct((m, d), F32),
        compiler_params=_cparams(("parallel", "arbitrary")),
        name="out_proj",
    )(x, mod3, mix_a, mix_b, w_a, w_b)


def _ffn_kernel(x_ref, nw_ref, sh_ref, sc_ref, ga_ref, fw_ref, wg_ref, wu_ref, wd_ref, o_ref, h_scr, acc_scr, *,
                final_norm):
    f = pl.program_id(1)

    @pl.when(f == 0)
    def _():
        h = _rms(x_ref[...]) * nw_ref[...]
        h_scr[...] = (h * (1.0 + sc_ref[0]) + sh_ref[0]).astype(BF16)
        acc_scr[...] = jnp.zeros_like(acc_scr)

    h = h_scr[...]
    g = jnp.dot(h, wg_ref[...], preferred_element_type=F32)
    u = jnp.dot(h, wu_ref[...], preferred_element_type=F32)
    acc_scr[...] += jnp.dot((_silu(g) * u).astype(BF16), wd_ref[...], preferred_element_type=F32)

    @pl.when(f == pl.num_programs(1) - 1)
    def _():
        y = x_ref[...] + ga_ref[0] * acc_scr[...]
        if final_norm:
            y = _rms(y) * fw_ref[...]
        o_ref[...] = y


def _ffn(x, nw, mod3, rows_per_group, fw, w_gate, w_up, w_down, final_norm):
    m, d = x.shape
    ff = w_gate.shape[1]
    tm = _tile(m, 512)
    tf = _tile(ff, 512)
    kern = functools.partial(_ffn_kernel, final_norm=final_norm)
    return pl.pallas_call(
        kern,
        grid=(m // tm, ff // tf),
        in_specs=[
            pl.BlockSpec((tm, d), lambda i, f: (i, 0)),
            pl.BlockSpec((1, d), lambda i, f: (0, 0)),
            _mod_spec(mod3, 3, d, tm, rows_per_group),
            _mod_spec(mod3, 4, d, tm, rows_per_group),
            _mod_spec(mod3, 5, d, tm, rows_per_group),
            pl.BlockSpec((1, d), lambda i, f: (0, 0)),
            pl.BlockSpec((d, tf), lambda i, f: (0, f)),
            pl.BlockSpec((d, tf), lambda i, f: (0, f)),
            pl.BlockSpec((tf, d), lambda i, f: (f, 0)),
        ],
        out_specs=pl.BlockSpec((tm, d), lambda i, f: (i, 0)),
        out_shape=jax.ShapeDtypeStruct((m, d), F32),
        scratch_shapes=[pltpu.VMEM((tm, d), BF16), pltpu.VMEM((tm, d), F32)],
        compiler_params=_cparams(("parallel", "arbitrary")),
        name="ffn",
    )(x, nw.reshape(1, d), mod3, mod3, mod3, fw.reshape(1, d), w_gate, w_up, w_down)


def kernel(x_prompt, x_sample, c_prompt, c_sample, cache_k, cache_v, state_gdn, state_conv, page_table, norm1_w, norm2_w, w_ada, b_ada, w_in, conv_w, a_log, dt_bias, gdn_norm_w, sb_bias, w_out, w_gate, w_up, w_down, final_norm_w):
    depth = w_in.shape[0]
    bp, seq, d = x_prompt.shape
    bs, seq_s, _ = x_sample.shape
    assert seq_s == 1, "the sample group advances one token per step"
    nh_g, nh_s = a_log.shape[1], sb_bias.shape[1]
    hd = gdn_norm_w.shape[1]
    gw, sw = nh_g * hd, nh_s * hd
    kw = conv_w.shape[1]
    c_ba = 4 * gw
    c_sb = c_ba + 2 * nh_g
    n_bat = -(-2 * nh_g // SUBLANES) * SUBLANES

    hp = x_prompt.reshape(bp * seq, d)
    hs = x_sample.reshape(bs, d)
    outs = {n: [] for n in ("kp", "vp", "gp", "cp", "ks", "vs", "gs", "cs")}
    for l in range(depth):
        w_main = jnp.concatenate([w_in[l][:, :c_ba], w_in[l][:, c_sb:]], axis=1).astype(BF16)
        w_ba = jnp.zeros((d, LANES), BF16).at[:, :2 * nh_g].set(w_in[l][:, c_ba:c_sb].astype(BF16))
        wo_a, wo_b = w_out[l][:gw].astype(BF16), w_out[l][gw:].astype(BF16)
        wg, wu, wd = w_gate[l].astype(BF16), w_up[l].astype(BF16), w_down[l].astype(BF16)
        last = l == depth - 1

        mod_p, mod_s = _ada(c_prompt, c_sample, w_ada[l], b_ada[l])
        mod_p3 = mod_p.reshape(bp, 1, 6 * d)
        mod_s3 = mod_s.reshape(1, bs, 6 * d)

        g_p, q_p, k_p, v_p, ba_p, bat_p = _inproj(hp, norm1_w[l], mod_p3, seq, w_main, w_ba, 4 * gw, sw, n_bat)
        gdn_o, gdn_s, conv_o = _gdn_chunked(
            g_p, ba_p, bat_p, jnp.zeros((bp, kw - 1, 3 * gw), F32), jnp.zeros((bp, nh_g, hd, hd), F32),
            conv_w[l], a_log[l], dt_bias[l], gdn_norm_w[l], bp, seq)
        sb_o = _sb_prompt(q_p, k_p, v_p, sb_bias[l], bp, seq, hd)
        hp = _outproj(hp, mod_p3, seq, gdn_o, sb_o, wo_a, wo_b)
        hp = _ffn(hp, norm2_w[l], mod_p3, seq, final_norm_w, wg, wu, wd, last)
        outs["kp"].append(k_p.reshape(bp, seq, nh_s, hd))
        outs["vp"].append(v_p.reshape(bp, seq, nh_s, hd))
        outs["gp"].append(gdn_s)
        outs["cp"].append(conv_o)

        g_s, q_s, k_s, v_s, ba_s = _inproj(hs, norm1_w[l], mod_s3, 1, w_main, w_ba, 4 * gw, sw, 0)
        gdn_o, gdn_s = _gdn_step(g_s, ba_s, state_conv[l], state_gdn[l], conv_w[l], a_log[l], dt_bias[l],
                                 gdn_norm_w[l])
        sb_o = _sb_paged(q_s.reshape(bs, nh_s, hd), k_s.reshape(bs, nh_s, hd), v_s.reshape(bs, nh_s, hd), sb_bias[l],
                         cache_k, cache_v, l, page_table)
        hs = _outproj(hs, mod_s3, 1, gdn_o, sb_o.reshape(bs, sw).astype(BF16), wo_a, wo_b)
        hs = _ffn(hs, norm2_w[l], mod_s3, 1, final_norm_w, wg, wu, wd, last)
        outs["ks"].append(k_s.reshape(bs, 1, nh_s, hd))
        outs["vs"].append(v_s.reshape(bs, 1, nh_s, hd))
        outs["gs"].append(gdn_s.astype(state_gdn.dtype))
        outs["cs"].append(jnp.concatenate([state_conv[l][:, 1:], g_s[:, None, :3 * gw]], axis=1).astype(state_conv.dtype))

    st = lambda n: jnp.stack(outs[n])
    return (hp.reshape(bp, seq, d), hs.reshape(bs, 1, d), st("kp"), st("vp"), st("gp"), st("cp"),
            st("ks"), st("vs"), st("gs"), st("cs"))
```

```python
import functools
import math

import jax
import jax.numpy as jnp
from jax import lax
from jax.experimental import pallas as pl
from jax.experimental.pallas import tpu as pltpu

F32 = jnp.float32
BF16 = jnp.bfloat16
NORM_EPS = 1e-6
LOG2E = 1.4426950408889634
LANES = 128
SUBLANES = 8
GDN_CHUNK = 128
INV_BLOCK = 16
SB_BLOCK = 256
SB_HEADS_PER_STEP = 4
PAGES_PER_STEP = 8
VMEM_LIMIT = 56 * 1024 * 1024


def _cparams(sem):
    return pltpu.CompilerParams(dimension_semantics=sem, vmem_limit_bytes=VMEM_LIMIT)


def _tile(n, pref):
    if n <= pref:
        return n
    t = pref
    while n % t:
        t //= 2
    return t


def _mm(a, b):
    return jnp.dot(a.astype(BF16), b.astype(BF16), preferred_element_type=F32)


def _mm_nt(a, b):
    return lax.dot_general(a.astype(BF16), b.astype(BF16), (((1,), (1,)), ((), ())), preferred_element_type=F32)


def _mm_tn(a, b):
    return lax.dot_general(a.astype(BF16), b.astype(BF16), (((0,), (0,)), ((), ())), preferred_element_type=F32)


def _split3(x):
    hi = x.astype(BF16)
    r = x - hi.astype(F32)
    mid = r.astype(BF16)
    lo = (r - mid.astype(F32)).astype(BF16)
    return hi, mid, lo


def _dot_01_rhs(x, m01):
    return sum(jnp.dot(p, m01, preferred_element_type=F32) for p in _split3(x))


def _dot_01_lhs(m01, x):
    return sum(jnp.dot(m01, p, preferred_element_type=F32) for p in _split3(x))


def _softplus(x):
    return jnp.maximum(x, 0.0) + jnp.log(1.0 + jnp.exp2(jnp.abs(x) * (-LOG2E)))


def _bmm(a, b):
    return lax.dot_general(a.astype(BF16), b.astype(BF16), (((2,), (1,)), ((0,), (0,))), preferred_element_type=F32)


def _bmm_nt(a, b):
    return lax.dot_general(a.astype(BF16), b.astype(BF16), (((2,), (2,)), ((0,), (0,))), preferred_element_type=F32)


def _bmm_tn(a, b):
    return lax.dot_general(a.astype(BF16), b.astype(BF16), (((1,), (1,)), ((0,), (0,))), preferred_element_type=F32)


def _silu(x):
    return x * jax.nn.sigmoid(x)


def _rms(x):
    return x * lax.rsqrt(jnp.mean(x * x, axis=-1, keepdims=True) + NORM_EPS)


def _ada_kernel(cp_ref, cs_ref, w_ref, b_ref, op_ref, os_ref):
    w = w_ref[...].astype(BF16)
    b = b_ref[...]
    for c_ref, o_ref in ((cp_ref, op_ref), (cs_ref, os_ref)):
        o_ref[...] = jnp.dot(_silu(c_ref[...]).astype(BF16), w, preferred_element_type=F32) + b


def _ada(c_p, c_s, w, b):
    d, n = w.shape
    tn = _tile(n, 1024)
    bp, bs = c_p.shape[0], c_s.shape[0]
    return pl.pallas_call(
        _ada_kernel,
        grid=(n // tn,),
        in_specs=[
            pl.BlockSpec((bp, d), lambda j: (0, 0)),
            pl.BlockSpec((bs, d), lambda j: (0, 0)),
            pl.BlockSpec((d, tn), lambda j: (0, j)),
            pl.BlockSpec((1, tn), lambda j: (0, j)),
        ],
        out_specs=[pl.BlockSpec((bp, tn), lambda j: (0, j)), pl.BlockSpec((bs, tn), lambda j: (0, j))],
        out_shape=[jax.ShapeDtypeStruct((bp, n), F32), jax.ShapeDtypeStruct((bs, n), F32)],
        compiler_params=_cparams(("parallel",)),
        name="ada_mod",
    )(c_p, c_s, w, b.reshape(1, n))


def _row_tile(m, mod3, rows_per_group, pref):
    return m if mod3.shape[1] != 1 else _tile(rows_per_group, pref)


def _mod_spec(mod3, comp, d, tm, rows_per_group, tn=None):
    per_row = mod3.shape[1] != 1
    assert not per_row or mod3.shape[1] == tm
    if tn is None:
        col = lambda j: comp
        tn = d
    else:
        col = lambda j: comp * (d // tn) + j
    if per_row:
        return pl.BlockSpec((1, tm, tn), lambda i, j: (0, 0, col(j)))
    return pl.BlockSpec((1, 1, tn), lambda i, j: (i * tm // rows_per_group, 0, col(j)))


def _inproj_kernel(x_ref, nw_ref, sc_ref, sh_ref, w_ref, wba_ref, og_ref, oq_ref, ok_ref, ov_ref, oba_ref, *rest,
                   n_g, n_h, n_bat):
    obat_ref = rest[0] if n_bat else None
    h_scr = rest[-1]
    j = pl.program_id(1)

    @pl.when(j == 0)
    def _():
        h = _rms(x_ref[...]) * nw_ref[...]
        h = (h * (1.0 + sc_ref[0]) + sh_ref[0]).astype(BF16)
        h_scr[...] = h
        ba = jnp.dot(h, wba_ref[...], preferred_element_type=F32)
        oba_ref[...] = ba
        if n_bat:
            obat_ref[...] = ba.T[:n_bat]

    acc = jnp.dot(h_scr[...], w_ref[...], preferred_element_type=F32)
    for ref, lo, hi in ((og_ref, 0, n_g), (oq_ref, n_g, n_g + n_h), (ok_ref, n_g + n_h, n_g + 2 * n_h),
                        (ov_ref, n_g + 2 * n_h, n_g + 3 * n_h)):
        @pl.when((j >= lo) & (j < hi))
        def _(ref=ref):
            ref[...] = acc


def _inproj(x, nw, mod3, rows_per_group, w_main, w_ba, gdn_cols, sb_cols, n_bat):
    m, d = x.shape
    tm = _row_tile(m, mod3, rows_per_group, 1024)
    tn = _tile(math.gcd(gdn_cols, sb_cols), 512)
    n_g, n_h = gdn_cols // tn, sb_cols // tn
    nj = n_g + 3 * n_h

    def seg(lo, n):
        return lambda i, j: (i, jnp.clip(j - lo, 0, n - 1))

    kern = functools.partial(_inproj_kernel, n_g=n_g, n_h=n_h, n_bat=n_bat)
    return pl.pallas_call(
        kern,
        grid=(m // tm, nj),
        in_specs=[
            pl.BlockSpec((tm, d), lambda i, j: (i, 0)),
            pl.BlockSpec((1, d), lambda i, j: (0, 0)),
            _mod_spec(mod3, 1, d, tm, rows_per_group),
            _mod_spec(mod3, 0, d, tm, rows_per_group),
            pl.BlockSpec((d, tn), lambda i, j: (0, j)),
            pl.BlockSpec((d, LANES), lambda i, j: (0, 0)),
        ],
        out_specs=[
            pl.BlockSpec((tm, tn), seg(0, n_g)),
            pl.BlockSpec((tm, tn), seg(n_g, n_h)),
            pl.BlockSpec((tm, tn), seg(n_g + n_h, n_h)),
            pl.BlockSpec((tm, tn), seg(n_g + 2 * n_h, n_h)),
            pl.BlockSpec((tm, LANES), lambda i, j: (i, 0)),
        ] + ([pl.BlockSpec((n_bat, tm), lambda i, j: (0, i))] if n_bat else []),
        out_shape=[
            jax.ShapeDtypeStruct((m, gdn_cols), F32),
            jax.ShapeDtypeStruct((m, sb_cols), F32),
            jax.ShapeDtypeStruct((m, sb_cols), F32),
            jax.ShapeDtypeStruct((m, sb_cols), F32),
            jax.ShapeDtypeStruct((m, LANES), F32),
        ] + ([jax.ShapeDtypeStruct((n_bat, m), F32)] if n_bat else []),
        scratch_shapes=[pltpu.VMEM((tm, d), BF16)],
        compiler_params=_cparams(("parallel", "arbitrary")),
        name="in_proj",
    )(x, nw.reshape(1, d), mod3, mod3, w_main, w_ba)


def _unit_lower_inverse_minus_eye(a, ri, ci):
    c = a.shape[-1]
    blk = ((ri // INV_BLOCK) == (ci // INV_BLOCK))[None]
    ad = jnp.where(blk, a, 0.0)
    off = a - ad
    xn = -ad
    p = ad
    for _ in range(int(math.log2(INV_BLOCK)) - 1):
        p = _bmm(p, p)
        xn = xn + p + _bmm(xn, p)
    m = off + _bmm(xn, off)
    yn = -m
    p = m
    for _ in range(int(math.log2(c // INV_BLOCK)) - 1):
        p = _bmm(p, p)
        yn = yn + p + _bmm(yn, p)
    return yn + xn + _bmm(yn, xn)


def _gdn_chunk_kernel(g_ref, ba_ref, bat_ref, cb_ref, s0_ref, cw_ref, arow_ref, drow_ref, acol_ref, dcol_ref, nw_ref,
                      o_ref, s_ref, hist_ref, *, nh, hd, chunk, kw):
    c = pl.program_id(1)

    @pl.when(c == 0)
    def _():
        s_ref[...] = s0_ref[...]
        hist_ref[...] = cb_ref[...]

    gw = nh * hd
    ri = lax.broadcasted_iota(jnp.int32, (chunk, chunk), 0)
    ci = lax.broadcasted_iota(jnp.int32, (chunk, chunk), 1)
    incl = ci <= ri
    strict = ci < ri
    ltri = incl.astype(BF16)
    utri = (ri <= ci).astype(BF16)

    ba = ba_ref[...]
    bat = bat_ref[...]
    beta_all = jax.nn.sigmoid(ba)
    g_all = -jnp.exp(arow_ref[...]) * _softplus(ba + drow_ref[...])
    g_t = -jnp.exp(acol_ref[...]) * _softplus(bat + dcol_ref[...])
    gc_all = _dot_01_lhs(ltri, g_all)
    gc_t = _dot_01_rhs(g_t, utri)
    scale = hd ** -0.5

    hist = hist_ref[0]
    x_all = g_ref[...]
    cw = cw_ref[...]
    nw = nw_ref[...]
    pad = hist.shape[0]

    xw = jnp.concatenate([hist, x_all[:, :3 * gw]], axis=0)
    y = cw[kw - 1:kw] * xw[pad:]
    for i in range(kw - 1):
        sft = kw - 1 - i
        y = y + cw[i:i + 1] * xw[pad - sft:pad - sft + chunk]
    y = _silu(y)

    def heads(arr, base):
        return jnp.stack([arr[:, base + h * hd:base + (h + 1) * hd] for h in range(nh)])

    def l2n(t):
        return t * lax.rsqrt(jnp.sum(t * t, axis=-1, keepdims=True) + NORM_EPS)

    q = l2n(heads(y, 0)) * scale
    k = l2n(heads(y, gw))
    v = heads(y, 2 * gw)
    beta = jnp.stack([beta_all[:, h:h + 1] for h in range(nh)])
    gc = jnp.stack([gc_all[:, nh + h:nh + h + 1] for h in range(nh)])
    gc_row = jnp.stack([gc_t[nh + h:nh + h + 1, :] for h in range(nh)])
    gl = gc[:, chunk - 1:chunk, :]
    decay = jnp.exp(jnp.where(incl[None], gc - gc_row, -1e30))
    kb = k * beta
    a = jnp.where(strict[None], _bmm_nt(kb, k) * decay, 0.0)
    tn = _unit_lower_inverse_minus_eye(a, ri, ci)
    egc = jnp.exp(gc)
    r = jnp.concatenate([v * beta, kb * egc], axis=2)
    uw = r + _bmm(tn, r)
    u, w = uw[:, :, :hd], uw[:, :, hd:]
    qk = jnp.where(incl[None], _bmm_nt(q, k) * decay, 0.0)
    s = s_ref[0]
    v_new = u - _bmm(w, s)
    o = _bmm(q * egc, s) + _bmm(qk, v_new)
    s_ref[0] = s * jnp.exp(gl) + _bmm_tn(k * jnp.exp(gl - gc), v_new)
    out = (_rms(o) * nw * _silu(heads(x_all, 3 * gw))).astype(o_ref.dtype)
    for h in range(nh):
        o_ref[:, h * hd:(h + 1) * hd] = out[h]

    hist_ref[0] = x_all[chunk - pad:, :3 * gw]


def _gdn_chunked(gdn, ba, bat, conv_buf, state0, conv_w, a_log, dt_bias, norm_w, batch, seq):
    m, gcols = gdn.shape
    nh = a_log.shape[0]
    hd = norm_w.shape[0]
    gw = nh * hd
    kw = conv_w.shape[0]
    chunk = _tile(seq, GDN_CHUNK)
    nc = seq // chunk
    n_bat = bat.shape[0]
    pad = SUBLANES
    cb = jnp.zeros((batch, pad, 3 * gw), F32).at[:, pad - (kw - 1):].set(conv_buf.astype(F32))
    cw = jnp.zeros((pad, 3 * gw), F32).at[:kw].set(conv_w.astype(F32))
    arow = jnp.zeros((1, LANES), F32).at[0, nh:2 * nh].set(a_log)
    drow = jnp.zeros((1, LANES), F32).at[0, nh:2 * nh].set(dt_bias)
    acol = jnp.zeros((n_bat, 1), F32).at[nh:2 * nh, 0].set(a_log)
    dcol = jnp.zeros((n_bat, 1), F32).at[nh:2 * nh, 0].set(dt_bias)
    kern = functools.partial(_gdn_chunk_kernel, nh=nh, hd=hd, chunk=chunk, kw=kw)
    const2 = lambda b, c: (0, 0)
    o, s, hist = pl.pallas_call(
        kern,
        grid=(batch, nc),
        in_specs=[
            pl.BlockSpec((chunk, gcols), lambda b, c: (b * nc + c, 0)),
            pl.BlockSpec((chunk, LANES), lambda b, c: (b * nc + c, 0)),
            pl.BlockSpec((n_bat, chunk), lambda b, c: (0, b * nc + c)),
            pl.BlockSpec((1, pad, 3 * gw), lambda b, c: (b, 0, 0)),
            pl.BlockSpec((1, nh, hd, hd), lambda b, c: (b, 0, 0, 0)),
            pl.BlockSpec((pad, 3 * gw), const2),
            pl.BlockSpec((1, LANES), const2),
            pl.BlockSpec((1, LANES), const2),
            pl.BlockSpec((n_bat, 1), const2),
            pl.BlockSpec((n_bat, 1), const2),
            pl.BlockSpec((1, hd), const2),
        ],
        out_specs=[
            pl.BlockSpec((chunk, gw), lambda b, c: (b * nc + c, 0)),
            pl.BlockSpec((1, nh, hd, hd), lambda b, c: (b, 0, 0, 0)),
            pl.BlockSpec((1, pad, 3 * gw), lambda b, c: (b, 0, 0)),
        ],
        out_shape=[
            jax.ShapeDtypeStruct((m, gw), BF16),
            jax.ShapeDtypeStruct((batch, nh, hd, hd), F32),
            jax.ShapeDtypeStruct((batch, pad, 3 * gw), F32),
        ],
        compiler_params=_cparams(("parallel", "arbitrary")),
        name="gdn_chunked",
    )(gdn, ba, bat, cb, state0.astype(F32), cw, arow, drow, acol, dcol, norm_w.reshape(1, hd).astype(F32))
    return o, s, hist[:, pad - (kw - 1):]


def _gdn_step_kernel(g_ref, ba_ref, cb_ref, s0_ref, cw_ref, arow_ref, drow_ref, nw_ref, o_ref, s_ref, *, nh, hd, kw):
    gw = nh * hd
    x = g_ref[0]
    cb = cb_ref[0]
    cw = cw_ref[...]
    y = cw[kw - 1:kw] * x[:, :3 * gw]
    for i in range(kw - 1):
        y = y + cw[i:i + 1] * cb[i:i + 1]
    y = _silu(y)
    ba = ba_ref[0]
    beta_all = jax.nn.sigmoid(ba)
    g_all = -jnp.exp(arow_ref[...]) * _softplus(ba + drow_ref[...])
    nw = nw_ref[...]
    row = lax.broadcasted_iota(jnp.int32, (SUBLANES, hd), 0)
    scale = hd ** -0.5

    def l2n(t):
        return t * lax.rsqrt(jnp.sum(t * t, axis=-1, keepdims=True) + NORM_EPS)

    for h in range(nh):
        q = l2n(y[:, h * hd:(h + 1) * hd]) * scale
        k = l2n(y[:, gw + h * hd:gw + (h + 1) * hd])
        v = y[:, 2 * gw + h * hd:2 * gw + (h + 1) * hd]
        beta = beta_all[:, h:h + 1]
        eg = jnp.exp(g_all[:, nh + h:nh + h + 1])
        s = s0_ref[0, h]
        lhs = jnp.where(row == 0, jnp.broadcast_to(k, (SUBLANES, hd)),
                        jnp.where(row == 1, jnp.broadcast_to(q, (SUBLANES, hd)), 0.0))
        ks_qs = _mm(lhs, s)
        ks, qs = ks_qs[0:1], ks_qs[1:2]
        v_new = beta * v - (beta * eg) * ks
        qk = jnp.sum(q * k, axis=-1, keepdims=True)
        o = eg * qs + qk * v_new
        k8 = jnp.where(row == 0, jnp.broadcast_to(k, (SUBLANES, hd)), 0.0)
        v8 = jnp.where(row == 0, jnp.broadcast_to(v_new, (SUBLANES, hd)), 0.0)
        s_ref[0, h] = s * eg + _mm_tn(k8, v8)
        z = x[:, 3 * gw + h * hd:3 * gw + (h + 1) * hd]
        o_ref[0, :, h * hd:(h + 1) * hd] = (_rms(o) * nw * _silu(z)).astype(o_ref.dtype)


def _gdn_step(gdn, ba, conv_buf, state0, conv_w, a_log, dt_bias, norm_w):
    b, gcols = gdn.shape
    nh = a_log.shape[0]
    hd = norm_w.shape[0]
    gw = nh * hd
    kw = conv_w.shape[0]
    arow = jnp.zeros((1, LANES), F32).at[0, nh:2 * nh].set(a_log)
    drow = jnp.zeros((1, LANES), F32).at[0, nh:2 * nh].set(dt_bias)
    kern = functools.partial(_gdn_step_kernel, nh=nh, hd=hd, kw=kw)
    const2 = lambda i: (0, 0)
    o, s = pl.pallas_call(
        kern,
        grid=(b,),
        in_specs=[
            pl.BlockSpec((1, 1, gcols), lambda i: (i, 0, 0)),
            pl.BlockSpec((1, 1, LANES), lambda i: (i, 0, 0)),
            pl.BlockSpec((1, kw - 1, 3 * gw), lambda i: (i, 0, 0)),
            pl.BlockSpec((1, nh, hd, hd), lambda i: (i, 0, 0, 0)),
            pl.BlockSpec((kw, 3 * gw), const2),
            pl.BlockSpec((1, LANES), const2),
            pl.BlockSpec((1, LANES), const2),
            pl.BlockSpec((1, hd), const2),
        ],
        out_specs=[
            pl.BlockSpec((1, 1, gw), lambda i: (i, 0, 0)),
            pl.BlockSpec((1, nh, hd, hd), lambda i: (i, 0, 0, 0)),
        ],
        out_shape=[jax.ShapeDtypeStruct((b, 1, gw), BF16), jax.ShapeDtypeStruct((b, nh, hd, hd), F32)],
        compiler_params=_cparams(("parallel",)),
        name="gdn_step",
    )(gdn.reshape(b, 1, gcols), ba.reshape(b, 1, LANES), conv_buf.astype(F32), state0.astype(F32),
      conv_w.astype(F32), arow, drow, norm_w.reshape(1, hd).astype(F32))
    return o.reshape(b, gw), s


def _sb_block_update(q, k, v, bias, carry, acc, mask, utri):
    g, tq, _ = q.shape
    tk = k.shape[1]
    zb = _bmm_nt(q, k)
    z = jnp.stack([zb[i] + bias[i] for i in range(g)])
    nlk = _softplus(z)
    if mask is not None:
        nlk = jnp.where(mask[None], nlk, 0.0)
    hi = nlk.astype(BF16)
    lo = (nlk - hi.astype(F32)).astype(BF16)
    hilo = jnp.concatenate([hi, lo], axis=2).reshape(g * tq, 2 * tk)
    later = jnp.dot(hilo, utri, preferred_element_type=F32).reshape(g, tq, tk)
    w = jnp.exp(z - nlk - later - carry)
    if mask is not None:
        w = jnp.where(mask[None], w, 0.0)
    acc = acc + _bmm(w, v)
    return carry + jnp.sum(nlk, axis=-1, keepdims=True), acc


def _sb_prompt_kernel(bias_ref, q_ref, k_ref, v_ref, o_ref, *, blk, hd, hp):
    qi = pl.program_id(2)
    hg = pl.program_id(1)
    bias = [bias_ref[hg * hp + i] for i in range(hp)]
    scale = hd ** -0.5
    q = jnp.stack([(q_ref[:, i * hd:(i + 1) * hd] * scale).astype(BF16) for i in range(hp)])
    ri = lax.broadcasted_iota(jnp.int32, (blk, blk), 0)
    ci = lax.broadcasted_iota(jnp.int32, (blk, blk), 1)
    utri = (ri > ci).astype(BF16)
    utri = jnp.concatenate([utri, utri], axis=0)

    def heads(ref, start):
        return jnp.stack([ref[pl.ds(start, blk), i * hd:(i + 1) * hd] for i in range(hp)])

    start = pl.multiple_of(qi * blk, blk)
    carry, acc = _sb_block_update(q, heads(k_ref, start), heads(v_ref, start), bias,
                                  jnp.zeros((hp, blk, 1), F32), jnp.zeros((hp, blk, hd), F32), ci < ri, utri)

    def body(it, ca):
        s0 = pl.multiple_of((qi - 1 - it) * blk, blk)
        return _sb_block_update(q, heads(k_ref, s0), heads(v_ref, s0), bias, ca[0], ca[1], None, utri)

    carry, acc = lax.fori_loop(0, qi, body, (carry, acc))
    for i in range(hp):
        o_ref[:, i * hd:(i + 1) * hd] = acc[i].astype(o_ref.dtype)


def _sb_prompt(q, k, v, bias, batch, seq, hd):
    m, width = q.shape
    nh = width // hd
    hp = _tile(nh, SB_HEADS_PER_STEP)
    blk = _tile(seq, SB_BLOCK)
    nq = seq // blk
    kern = functools.partial(_sb_prompt_kernel, blk=blk, hd=hd, hp=hp)
    return pl.pallas_call(
        kern,
        grid=(batch, nh // hp, nq),
        in_specs=[
            pl.BlockSpec(memory_space=pltpu.SMEM),
            pl.BlockSpec((blk, hp * hd), lambda b, h, i: (b * nq + i, h)),
            pl.BlockSpec((seq, hp * hd), lambda b, h, i: (b, h)),
            pl.BlockSpec((seq, hp * hd), lambda b, h, i: (b, h)),
        ],
        out_specs=pl.BlockSpec((blk, hp * hd), lambda b, h, i: (b * nq + i, h)),
        out_shape=jax.ShapeDtypeStruct((m, width), BF16),
        compiler_params=_cparams(("parallel", "parallel", "arbitrary")),
        name="sb_prompt",
    )(bias.astype(F32), q, k, v)


def _sb_paged_kernel(pt_ref, q_ref, knew_ref, vnew_ref, brow_ref, bcol_ref, *refs, nh, hd, page, npp, n_pages):
    k_refs, v_refs = refs[:npp], refs[npp:2 * npp]
    o_ref, carry_ref = refs[2 * npp], refs[2 * npp + 1]
    s = pl.program_id(1)
    width = page * nh
    nt = width // LANES
    sub = lax.broadcasted_iota(jnp.int32, (nh, width), 0)
    lane = lax.broadcasted_iota(jnp.int32, (nh, width), 1)
    own = (lane % nh) == sub
    ri = lax.broadcasted_iota(jnp.int32, (LANES, 2 * LANES), 0)
    ci = lax.broadcasted_iota(jnp.int32, (LANES, 2 * LANES), 1)
    usum = ((ri > ci) | (ci >= LANES)).astype(BF16)
    q = (q_ref[0] * (hd ** -0.5)).astype(BF16)
    brow = brow_ref[...]

    @pl.when(s == 0)
    def _():
        z = jnp.sum(q_ref[0] * knew_ref[0], axis=-1, keepdims=True) * (hd ** -0.5) + bcol_ref[...]
        pos = jnp.full((nh, 1), n_pages * page, jnp.int32)
        visible = pos < pos
        nlk = jnp.where(visible, _softplus(z), 0.0)
        w = jnp.where(visible, jnp.exp(z - nlk), 0.0)
        o_ref[0] = w * vnew_ref[0]
        carry_ref[...] = jnp.broadcast_to(nlk, (nh, LANES))

    z = [_mm_nt(q, k_refs[p][...]) + brow for p in range(npp)]
    nlk = [jnp.where(own, _softplus(zp), 0.0) for zp in z]
    tiles = jnp.concatenate([n[:, t * LANES:(t + 1) * LANES] for n in nlk for t in range(nt)], axis=0)
    hi = tiles.astype(BF16)
    lo = (tiles - hi.astype(F32)).astype(BF16)
    su = jnp.dot(hi, usum, preferred_element_type=F32) + jnp.dot(lo, usum, preferred_element_type=F32)
    run = carry_ref[...]
    acc = o_ref[0]
    for p in range(npp):
        later = []
        for t in reversed(range(nt)):
            r0 = (p * nt + t) * nh
            later.append(su[r0:r0 + nh, :LANES] + run)
            run = run + su[r0:r0 + nh, LANES:]
        later = jnp.concatenate(later[::-1], axis=1)
        w = jnp.where(own, jnp.exp(z[p] - nlk[p] - later), 0.0)
        acc = acc + _mm(w, v_refs[p][...])
    o_ref[0] = acc
    carry_ref[...] = run


def _sb_paged(q, k_new, v_new, bias, cache_k, cache_v, layer, page_table):
    b, nh, hd = q.shape
    n_pool, page = cache_k.shape[1], cache_k.shape[2]
    n_pages = page_table.shape[1]
    npp = _tile(n_pages, PAGES_PER_STEP)
    width = page * nh
    ck = cache_k.reshape(cache_k.shape[0], n_pool, width, hd)
    cv = cache_v.reshape(cache_v.shape[0], n_pool, width, hd)
    brow = jnp.tile(bias.astype(F32), page).reshape(1, width)

    def page_spec(p):
        return pl.BlockSpec((None, None, width, hd),
                            lambda i, s, pt: (layer, pt[i * n_pages + n_pages - 1 - (s * npp + p)], 0, 0))

    kern = functools.partial(_sb_paged_kernel, nh=nh, hd=hd, page=page, npp=npp, n_pages=n_pages)
    head_spec = pl.BlockSpec((1, nh, hd), lambda i, s, pt: (i, 0, 0))
    grid_spec = pltpu.PrefetchScalarGridSpec(
        num_scalar_prefetch=1,
        grid=(b, n_pages // npp),
        in_specs=[head_spec, head_spec, head_spec, pl.BlockSpec((1, width), lambda i, s, pt: (0, 0)),
                  pl.BlockSpec((nh, 1), lambda i, s, pt: (0, 0))]
        + [page_spec(p) for p in range(npp)] + [page_spec(p) for p in range(npp)],
        out_specs=head_spec,
        scratch_shapes=[pltpu.VMEM((nh, LANES), F32)],
    )
    return pl.pallas_call(
        kern,
        grid_spec=grid_spec,
        out_shape=jax.ShapeDtypeStruct((b, nh, hd), F32),
        compiler_params=_cparams(("parallel", "arbitrary")),
        name="sb_paged",
    )(page_table.reshape(-1).astype(jnp.int32), q, k_new, v_new, brow, bias.astype(F32).reshape(nh, 1),
      *([ck] * npp), *([cv] * npp))


def _outproj_kernel(x_ref, ga_ref, a_ref, b_ref, wa_ref, wb_ref, o_ref):
    y = jnp.dot(a_ref[...], wa_ref[...], preferred_element_type=F32)
    y = y + jnp.dot(b_ref[...], wb_ref[...], preferred_element_type=F32)
    o_ref[...] = x_ref[...] + ga_ref[0] * y


def _outproj(x, mod3, rows_per_group, mix_a, mix_b, w_a, w_b):
    m, d = x.shape
    ka, kb = mix_a.shape[1], mix_b.shape[1]
    tm = _row_tile(m, mod3, rows_per_group, 512)
    tn = d
    return pl.pallas_call(
        _outproj_kernel,
        grid=(m // tm, d // tn),
        in_specs=[
            pl.BlockSpec((tm, tn), lambda i, j: (i, j)),
            _mod_spec(mod3, 2, d, tm, rows_per_group, tn),
            pl.BlockSpec((tm, ka), lambda i, j: (i, 0)),
            pl.BlockSpec((tm, kb), lambda i, j: (i, 0)),
            pl.BlockSpec((ka, tn), lambda i, j: (0, j)),
            pl.BlockSpec((kb, tn), lambda i, j: (0, j)),
        ],
        out_specs=pl.BlockSpec((tm, tn), lambda i, j: (i, j)),
        out_shape=jax.ShapeDtypeStruct((m, d), F32),
        compiler_params=_cparams(("parallel", "arbitrary")),
        name="out_proj",
    )(x, mod3, mix_a, mix_b, w_a, w_b)


def _ffn_kernel(x_ref, nw_ref, sh_ref, sc_ref, ga_ref, fw_ref, wg_ref, wu_ref, wd_ref, o_ref, h_scr, acc_scr, *,
                final_norm):
    f = pl.program_id(1)

    @pl.when(f == 0)
    def _():
        h = _rms(x_ref[...]) * nw_ref[...]
        h_scr[...] = (h * (1.0 + sc_ref[0]) + sh_ref[0]).astype(BF16)
        acc_scr[...] = jnp.zeros_like(acc_scr)

    h = h_scr[...]
    g = jnp.dot(h, wg_ref[...], preferred_element_type=F32)
    u = jnp.dot(h, wu_ref[...], preferred_element_type=F32)
    acc_scr[...] += jnp.dot((_silu(g) * u).astype(BF16), wd_ref[...], preferred_element_type=F32)

    @pl.when(f == pl.num_programs(1) - 1)
    def _():
        y = x_ref[...] + ga_ref[0] * acc_scr[...]
        if final_norm:
            y = _rms(y) * fw_ref[...]
        o_ref[...] = y


def _ffn(x, nw, mod3, rows_per_group, fw, w_gate, w_up, w_down, final_norm):
    m, d = x.shape
    ff = w_gate.shape[1]
    tm = _row_tile(m, mod3, rows_per_group, 512)
    tf = _tile(ff, 512)
    kern = functools.partial(_ffn_kernel, final_norm=final_norm)
    return pl.pallas_call(
        kern,
        grid=(m // tm, ff // tf),
        in_specs=[
            pl.BlockSpec((tm, d), lambda i, f: (i, 0)),
            pl.BlockSpec((1, d), lambda i, f: (0, 0)),
            _mod_spec(mod3, 3, d, tm, rows_per_group),
            _mod_spec(mod3, 4, d, tm, rows_per_group),
            _mod_spec(mod3, 5, d, tm, rows_per_group),
            pl.BlockSpec((1, d), lambda i, f: (0, 0)),
            pl.BlockSpec((d, tf), lambda i, f: (0, f)),
            pl.BlockSpec((d, tf), lambda i, f: (0, f)),
            pl.BlockSpec((tf, d), lambda i, f: (f, 0)),
        ],
        out_specs=pl.BlockSpec((tm, d), lambda i, f: (i, 0)),
        out_shape=jax.ShapeDtypeStruct((m, d), F32),
        scratch_shapes=[pltpu.VMEM((tm, d), BF16), pltpu.VMEM((tm, d), F32)],
        compiler_params=_cparams(("parallel", "arbitrary")),
        name="ffn",
    )(x, nw.reshape(1, d), mod3, mod3, mod3, fw.reshape(1, d), w_gate, w_up, w_down)


def kernel(x_prompt, x_sample, c_prompt, c_sample, cache_k, cache_v, state_gdn, state_conv, page_table, norm1_w, norm2_w, w_ada, b_ada, w_in, conv_w, a_log, dt_bias, gdn_norm_w, sb_bias, w_out, w_gate, w_up, w_down, final_norm_w):
    depth = w_in.shape[0]
    bp, seq, d = x_prompt.shape
    bs, seq_s, _ = x_sample.shape
    assert seq_s == 1, "the sample group advances one token per step"
    nh_g, nh_s = a_log.shape[1], sb_bias.shape[1]
    hd = gdn_norm_w.shape[1]
    gw, sw = nh_g * hd, nh_s * hd
    kw = conv_w.shape[1]
    c_ba = 4 * gw
    c_sb = c_ba + 2 * nh_g
    n_bat = -(-2 * nh_g // SUBLANES) * SUBLANES

    hp = x_prompt.reshape(bp * seq, d)
    hs = x_sample.reshape(bs, d)
    outs = {n: [] for n in ("kp", "vp", "gp", "cp", "ks", "vs", "gs", "cs")}
    for l in range(depth):
        w_main = jnp.concatenate([w_in[l][:, :c_ba], w_in[l][:, c_sb:]], axis=1).astype(BF16)
        w_ba = jnp.zeros((d, LANES), BF16).at[:, :2 * nh_g].set(w_in[l][:, c_ba:c_sb].astype(BF16))
        wo_a, wo_b = w_out[l][:gw].astype(BF16), w_out[l][gw:].astype(BF16)
        wg, wu, wd = w_gate[l].astype(BF16), w_up[l].astype(BF16), w_down[l].astype(BF16)
        last = l == depth - 1

        mod_p, mod_s = _ada(c_prompt, c_sample, w_ada[l], b_ada[l])
        mod_p3 = mod_p.reshape(bp, 1, 6 * d)
        mod_s3 = mod_s.reshape(1, bs, 6 * d)

        g_p, q_p, k_p, v_p, ba_p, bat_p = _inproj(hp, norm1_w[l], mod_p3, seq, w_main, w_ba, 4 * gw, sw, n_bat)
        gdn_o, gdn_s, conv_o = _gdn_chunked(
            g_p, ba_p, bat_p, jnp.zeros((bp, kw - 1, 3 * gw), F32), jnp.zeros((bp, nh_g, hd, hd), F32),
            conv_w[l], a_log[l], dt_bias[l], gdn_norm_w[l], bp, seq)
        sb_o = _sb_prompt(q_p, k_p, v_p, sb_bias[l], bp, seq, hd)
        hp = _outproj(hp, mod_p3, seq, gdn_o, sb_o, wo_a, wo_b)
        hp = _ffn(hp, norm2_w[l], mod_p3, seq, final_norm_w, wg, wu, wd, last)
        outs["kp"].append(k_p.reshape(bp, seq, nh_s, hd))
        outs["vp"].append(v_p.reshape(bp, seq, nh_s, hd))
        outs["gp"].append(gdn_s)
        outs["cp"].append(conv_o)

        g_s, q_s, k_s, v_s, ba_s = _inproj(hs, norm1_w[l], mod_s3, 1, w_main, w_ba, 4 * gw, sw, 0)
        gdn_o, gdn_s = _gdn_step(g_s, ba_s, state_conv[l], state_gdn[l], conv_w[l], a_log[l], dt_bias[l],
                                 gdn_norm_w[l])
        sb_o = _sb_paged(q_s.reshape(bs, nh_s, hd), k_s.reshape(bs, nh_s, hd), v_s.reshape(bs, nh_s, hd), sb_bias[l],
                         cache_k, cache_v, l, page_table)
        hs = _outproj(hs, mod_s3, 1, gdn_o, sb_o.reshape(bs, sw).astype(BF16), wo_a, wo_b)
        hs = _ffn(hs, norm2_w[l], mod_s3, 1, final_norm_w, wg, wu, wd, last)
        outs["ks"].append(k_s.reshape(bs, 1, nh_s, hd))
        outs["vs"].append(v_s.reshape(bs, 1, nh_s, hd))
        outs["gs"].append(gdn_s.astype(state_gdn.dtype))
        outs["cs"].append(jnp.concatenate([state_conv[l][:, 1:], g_s[:, None, :3 * gw]], axis=1).astype(state_conv.dtype))

    st = lambda n: jnp.stack(outs[n])
    return (hp.reshape(bp, seq, d), hs.reshape(bs, 1, d), st("kp"), st("vp"), st("gp"), st("cp"),
            st("ks"), st("vs"), st("gs"), st("cs"))
```

```python
import functools
import math

import jax
import jax.numpy as jnp
from jax import lax
from jax.experimental import pallas as pl
from jax.experimental.pallas import tpu as pltpu

F32 = jnp.float32
BF16 = jnp.bfloat16
NORM_EPS = 1e-6
LOG2E = 1.4426950408889634
LANES = 128
SUBLANES = 8
GDN_CHUNK = 128
INV_BLOCK = 16
SB_BLOCK = 256
SB_HEADS_PER_STEP = 4
PAGES_PER_STEP = 8
VMEM_LIMIT = 56 * 1024 * 1024


def _cparams(sem):
    return pltpu.CompilerParams(dimension_semantics=sem, vmem_limit_bytes=VMEM_LIMIT)


def _tile(n, pref):
    if n <= pref:
        return n
    t = pref
    while n % t:
        t //= 2
    return t


def _mm(a, b):
    return jnp.dot(a.astype(BF16), b.astype(BF16), preferred_element_type=F32)


def _mm_nt(a, b):
    return lax.dot_general(a.astype(BF16), b.astype(BF16), (((1,), (1,)), ((), ())), preferred_element_type=F32)


def _mm_tn(a, b):
    return lax.dot_general(a.astype(BF16), b.astype(BF16), (((0,), (0,)), ((), ())), preferred_element_type=F32)


def _split3(x):
    hi = x.astype(BF16)
    r = x - hi.astype(F32)
    mid = r.astype(BF16)
    lo = (r - mid.astype(F32)).astype(BF16)
    return hi, mid, lo


def _dot_01_rhs(x, m01):
    return sum(jnp.dot(p, m01, preferred_element_type=F32) for p in _split3(x))


def _dot_01_lhs(m01, x):
    return sum(jnp.dot(m01, p, preferred_element_type=F32) for p in _split3(x))


def _softplus(x):
    return jnp.maximum(x, 0.0) + jnp.log(1.0 + jnp.exp2(jnp.abs(x) * (-LOG2E)))


def _bmm(a, b):
    return lax.dot_general(a.astype(BF16), b.astype(BF16), (((2,), (1,)), ((0,), (0,))), preferred_element_type=F32)


def _bmm_nt(a, b):
    return lax.dot_general(a.astype(BF16), b.astype(BF16), (((2,), (2,)), ((0,), (0,))), preferred_element_type=F32)


def _bmm_tn(a, b):
    return lax.dot_general(a.astype(BF16), b.astype(BF16), (((1,), (1,)), ((0,), (0,))), preferred_element_type=F32)


def _silu(x):
    return x * jax.nn.sigmoid(x)


def _rms(x):
    return x * lax.rsqrt(jnp.mean(x * x, axis=-1, keepdims=True) + NORM_EPS)


def _ada_kernel(cp_ref, cs_ref, w_ref, b_ref, op_ref, os_ref):
    w = w_ref[...].astype(BF16)
    b = b_ref[...]
    for c_ref, o_ref in ((cp_ref, op_ref), (cs_ref, os_ref)):
        o_ref[...] = jnp.dot(_silu(c_ref[...]).astype(BF16), w, preferred_element_type=F32) + b


def _ada(c_p, c_s, w, b):
    d, n = w.shape
    tn = _tile(n, 1024)
    bp, bs = c_p.shape[0], c_s.shape[0]
    return pl.pallas_call(
        _ada_kernel,
        grid=(n // tn,),
        in_specs=[
            pl.BlockSpec((bp, d), lambda j: (0, 0)),
            pl.BlockSpec((bs, d), lambda j: (0, 0)),
            pl.BlockSpec((d, tn), lambda j: (0, j)),
            pl.BlockSpec((1, tn), lambda j: (0, j)),
        ],
        out_specs=[pl.BlockSpec((bp, tn), lambda j: (0, j)), pl.BlockSpec((bs, tn), lambda j: (0, j))],
        out_shape=[jax.ShapeDtypeStruct((bp, n), F32), jax.ShapeDtypeStruct((bs, n), F32)],
        compiler_params=_cparams(("parallel",)),
        name="ada_mod",
    )(c_p, c_s, w, b.reshape(1, n))


def _row_tile(m, mod3, rows_per_group, pref):
    return m if mod3.shape[1] != 1 else _tile(rows_per_group, pref)


def _mod_spec(mod3, comp, d, tm, rows_per_group, tn=None):
    per_row = mod3.shape[1] != 1
    assert not per_row or mod3.shape[1] == tm
    if tn is None:
        col = lambda j: comp
        tn = d
    else:
        col = lambda j: comp * (d // tn) + j
    if per_row:
        return pl.BlockSpec((1, tm, tn), lambda i, j, *_: (0, 0, col(j)))
    return pl.BlockSpec((1, 1, tn), lambda i, j, *_: (i * tm // rows_per_group, 0, col(j)))


def _inproj_kernel(x_ref, nw_ref, sc_ref, sh_ref, w_ref, wba_ref, og_ref, oq_ref, ok_ref, ov_ref, oba_ref, *rest,
                   n_g, n_h, n_bat):
    obat_ref = rest[0] if n_bat else None
    h_scr = rest[-1]
    j = pl.program_id(1)

    @pl.when(j == 0)
    def _():
        h = _rms(x_ref[...]) * nw_ref[...]
        h = (h * (1.0 + sc_ref[0]) + sh_ref[0]).astype(BF16)
        h_scr[...] = h
        ba = jnp.dot(h, wba_ref[...], preferred_element_type=F32)
        oba_ref[...] = ba
        if n_bat:
            obat_ref[...] = ba.T[:n_bat]

    acc = jnp.dot(h_scr[...], w_ref[...], preferred_element_type=F32)
    for ref, lo, hi in ((og_ref, 0, n_g), (oq_ref, n_g, n_g + n_h), (ok_ref, n_g + n_h, n_g + 2 * n_h),
                        (ov_ref, n_g + 2 * n_h, n_g + 3 * n_h)):
        @pl.when((j >= lo) & (j < hi))
        def _(ref=ref):
            ref[...] = acc


def _inproj(x, nw, mod3, rows_per_group, w_main, w_ba, gdn_cols, sb_cols, n_bat):
    m, d = x.shape
    tm = _row_tile(m, mod3, rows_per_group, 1024)
    tn = _tile(math.gcd(gdn_cols, sb_cols), 512)
    n_g, n_h = gdn_cols // tn, sb_cols // tn
    nj = n_g + 3 * n_h

    def seg(lo, n):
        return lambda i, j: (i, jnp.clip(j - lo, 0, n - 1))

    kern = functools.partial(_inproj_kernel, n_g=n_g, n_h=n_h, n_bat=n_bat)
    return pl.pallas_call(
        kern,
        grid=(m // tm, nj),
        in_specs=[
            pl.BlockSpec((tm, d), lambda i, j: (i, 0)),
            pl.BlockSpec((1, d), lambda i, j: (0, 0)),
            _mod_spec(mod3, 1, d, tm, rows_per_group),
            _mod_spec(mod3, 0, d, tm, rows_per_group),
            pl.BlockSpec((d, tn), lambda i, j: (0, j)),
            pl.BlockSpec((d, LANES), lambda i, j: (0, 0)),
        ],
        out_specs=[
            pl.BlockSpec((tm, tn), seg(0, n_g)),
            pl.BlockSpec((tm, tn), seg(n_g, n_h)),
            pl.BlockSpec((tm, tn), seg(n_g + n_h, n_h)),
            pl.BlockSpec((tm, tn), seg(n_g + 2 * n_h, n_h)),
            pl.BlockSpec((tm, LANES), lambda i, j: (i, 0)),
        ] + ([pl.BlockSpec((n_bat, tm), lambda i, j: (0, i))] if n_bat else []),
        out_shape=[
            jax.ShapeDtypeStruct((m, gdn_cols), F32),
            jax.ShapeDtypeStruct((m, sb_cols), F32),
            jax.ShapeDtypeStruct((m, sb_cols), F32),
            jax.ShapeDtypeStruct((m, sb_cols), F32),
            jax.ShapeDtypeStruct((m, LANES), F32),
        ] + ([jax.ShapeDtypeStruct((n_bat, m), F32)] if n_bat else []),
        scratch_shapes=[pltpu.VMEM((tm, d), BF16)],
        compiler_params=_cparams(("parallel", "arbitrary")),
        name="in_proj",
    )(x, nw.reshape(1, d), mod3, mod3, w_main, w_ba)


def _unit_lower_inverse_minus_eye(a, ri, ci):
    c = a.shape[-1]
    blk = ((ri // INV_BLOCK) == (ci // INV_BLOCK))[None]
    ad = jnp.where(blk, a, 0.0)
    off = a - ad
    xn = -ad
    p = ad
    for _ in range(int(math.log2(INV_BLOCK)) - 1):
        p = _bmm(p, p)
        xn = xn + p + _bmm(xn, p)
    m = off + _bmm(xn, off)
    yn = -m
    p = m
    for _ in range(int(math.log2(c // INV_BLOCK)) - 1):
        p = _bmm(p, p)
        yn = yn + p + _bmm(yn, p)
    return yn + xn + _bmm(yn, xn)


def _gdn_chunk_kernel(g_ref, ba_ref, bat_ref, cb_ref, s0_ref, cw_ref, arow_ref, drow_ref, acol_ref, dcol_ref, nw_ref,
                      o_ref, s_ref, hist_ref, *, nh, hd, chunk, kw):
    c = pl.program_id(1)

    @pl.when(c == 0)
    def _():
        s_ref[...] = s0_ref[...]
        hist_ref[...] = cb_ref[...]

    gw = nh * hd
    ri = lax.broadcasted_iota(jnp.int32, (chunk, chunk), 0)
    ci = lax.broadcasted_iota(jnp.int32, (chunk, chunk), 1)
    incl = ci <= ri
    strict = ci < ri
    ltri = incl.astype(BF16)
    utri = (ri <= ci).astype(BF16)

    ba = ba_ref[...]
    bat = bat_ref[...]
    beta_all = jax.nn.sigmoid(ba)
    g_all = -jnp.exp(arow_ref[...]) * _softplus(ba + drow_ref[...])
    g_t = -jnp.exp(acol_ref[...]) * _softplus(bat + dcol_ref[...])
    gc_all = _dot_01_lhs(ltri, g_all)
    gc_t = _dot_01_rhs(g_t, utri)
    scale = hd ** -0.5

    hist = hist_ref[0]
    x_all = g_ref[...]
    cw = cw_ref[...]
    nw = nw_ref[...]
    pad = hist.shape[0]

    xw = jnp.concatenate([hist, x_all[:, :3 * gw]], axis=0)
    y = cw[kw - 1:kw] * xw[pad:]
    for i in range(kw - 1):
        sft = kw - 1 - i
        y = y + cw[i:i + 1] * xw[pad - sft:pad - sft + chunk]
    y = _silu(y)

    def heads(arr, base):
        return jnp.stack([arr[:, base + h * hd:base + (h + 1) * hd] for h in range(nh)])

    def l2n(t):
        return t * lax.rsqrt(jnp.sum(t * t, axis=-1, keepdims=True) + NORM_EPS)

    q = l2n(heads(y, 0)) * scale
    k = l2n(heads(y, gw))
    v = heads(y, 2 * gw)
    beta = jnp.stack([beta_all[:, h:h + 1] for h in range(nh)])
    gc = jnp.stack([gc_all[:, nh + h:nh + h + 1] for h in range(nh)])
    gc_row = jnp.stack([gc_t[nh + h:nh + h + 1, :] for h in range(nh)])
    gl = gc[:, chunk - 1:chunk, :]
    decay = jnp.exp(jnp.where(incl[None], gc - gc_row, -1e30))
    kb = k * beta
    a = jnp.where(strict[None], _bmm_nt(kb, k) * decay, 0.0)
    tn = _unit_lower_inverse_minus_eye(a, ri, ci)
    egc = jnp.exp(gc)
    r = jnp.concatenate([v * beta, kb * egc], axis=2)
    uw = r + _bmm(tn, r)
    u, w = uw[:, :, :hd], uw[:, :, hd:]
    qk = jnp.where(incl[None], _bmm_nt(q, k) * decay, 0.0)
    s = s_ref[0]
    v_new = u - _bmm(w, s)
    o = _bmm(q * egc, s) + _bmm(qk, v_new)
    s_ref[0] = s * jnp.exp(gl) + _bmm_tn(k * jnp.exp(gl - gc), v_new)
    out = (_rms(o) * nw * _silu(heads(x_all, 3 * gw))).astype(o_ref.dtype)
    for h in range(nh):
        o_ref[:, h * hd:(h + 1) * hd] = out[h]

    hist_ref[0] = x_all[chunk - pad:, :3 * gw]


def _gdn_chunked(gdn, ba, bat, conv_buf, state0, conv_w, a_log, dt_bias, norm_w, batch, seq):
    m, gcols = gdn.shape
    nh = a_log.shape[0]
    hd = norm_w.shape[0]
    gw = nh * hd
    kw = conv_w.shape[0]
    chunk = _tile(seq, GDN_CHUNK)
    nc = seq // chunk
    n_bat = bat.shape[0]
    pad = SUBLANES
    cb = jnp.zeros((batch, pad, 3 * gw), F32).at[:, pad - (kw - 1):].set(conv_buf.astype(F32))
    cw = jnp.zeros((pad, 3 * gw), F32).at[:kw].set(conv_w.astype(F32))
    arow = jnp.zeros((1, LANES), F32).at[0, nh:2 * nh].set(a_log)
    drow = jnp.zeros((1, LANES), F32).at[0, nh:2 * nh].set(dt_bias)
    acol = jnp.zeros((n_bat, 1), F32).at[nh:2 * nh, 0].set(a_log)
    dcol = jnp.zeros((n_bat, 1), F32).at[nh:2 * nh, 0].set(dt_bias)
    kern = functools.partial(_gdn_chunk_kernel, nh=nh, hd=hd, chunk=chunk, kw=kw)
    const2 = lambda b, c: (0, 0)
    o, s, hist = pl.pallas_call(
        kern,
        grid=(batch, nc),
        in_specs=[
            pl.BlockSpec((chunk, gcols), lambda b, c: (b * nc + c, 0)),
            pl.BlockSpec((chunk, LANES), lambda b, c: (b * nc + c, 0)),
            pl.BlockSpec((n_bat, chunk), lambda b, c: (0, b * nc + c)),
            pl.BlockSpec((1, pad, 3 * gw), lambda b, c: (b, 0, 0)),
            pl.BlockSpec((1, nh, hd, hd), lambda b, c: (b, 0, 0, 0)),
            pl.BlockSpec((pad, 3 * gw), const2),
            pl.BlockSpec((1, LANES), const2),
            pl.BlockSpec((1, LANES), const2),
            pl.BlockSpec((n_bat, 1), const2),
            pl.BlockSpec((n_bat, 1), const2),
            pl.BlockSpec((1, hd), const2),
        ],
        out_specs=[
            pl.BlockSpec((chunk, gw), lambda b, c: (b * nc + c, 0)),
            pl.BlockSpec((1, nh, hd, hd), lambda b, c: (b, 0, 0, 0)),
            pl.BlockSpec((1, pad, 3 * gw), lambda b, c: (b, 0, 0)),
        ],
        out_shape=[
            jax.ShapeDtypeStruct((m, gw), BF16),
            jax.ShapeDtypeStruct((batch, nh, hd, hd), F32),
            jax.ShapeDtypeStruct((batch, pad, 3 * gw), F32),
        ],
        compiler_params=_cparams(("parallel", "arbitrary")),
        name="gdn_chunked",
    )(gdn, ba, bat, cb, state0.astype(F32), cw, arow, drow, acol, dcol, norm_w.reshape(1, hd).astype(F32))
    return o, s, hist[:, pad - (kw - 1):]


def _gdn_step_kernel(g_ref, ba_ref, cb_ref, s0_ref, cw_ref, arow_ref, drow_ref, nw_ref, o_ref, s_ref, *, nh, hd, kw):
    gw = nh * hd
    x = g_ref[0]
    cb = cb_ref[0]
    cw = cw_ref[...]
    y = cw[kw - 1:kw] * x[:, :3 * gw]
    for i in range(kw - 1):
        y = y + cw[i:i + 1] * cb[i:i + 1]
    y = _silu(y)
    ba = ba_ref[0]
    beta_all = jax.nn.sigmoid(ba)
    g_all = -jnp.exp(arow_ref[...]) * _softplus(ba + drow_ref[...])
    nw = nw_ref[...]
    row = lax.broadcasted_iota(jnp.int32, (SUBLANES, hd), 0)
    scale = hd ** -0.5

    def l2n(t):
        return t * lax.rsqrt(jnp.sum(t * t, axis=-1, keepdims=True) + NORM_EPS)

    for h in range(nh):
        q = l2n(y[:, h * hd:(h + 1) * hd]) * scale
        k = l2n(y[:, gw + h * hd:gw + (h + 1) * hd])
        v = y[:, 2 * gw + h * hd:2 * gw + (h + 1) * hd]
        beta = beta_all[:, h:h + 1]
        eg = jnp.exp(g_all[:, nh + h:nh + h + 1])
        s = s0_ref[0, h]
        lhs = jnp.where(row == 0, jnp.broadcast_to(k, (SUBLANES, hd)),
                        jnp.where(row == 1, jnp.broadcast_to(q, (SUBLANES, hd)), 0.0))
        ks_qs = _mm(lhs, s)
        ks, qs = ks_qs[0:1], ks_qs[1:2]
        v_new = beta * v - (beta * eg) * ks
        qk = jnp.sum(q * k, axis=-1, keepdims=True)
        o = eg * qs + qk * v_new
        k8 = jnp.where(row == 0, jnp.broadcast_to(k, (SUBLANES, hd)), 0.0)
        v8 = jnp.where(row == 0, jnp.broadcast_to(v_new, (SUBLANES, hd)), 0.0)
        s_ref[0, h] = s * eg + _mm_tn(k8, v8)
        z = x[:, 3 * gw + h * hd:3 * gw + (h + 1) * hd]
        o_ref[0, :, h * hd:(h + 1) * hd] = (_rms(o) * nw * _silu(z)).astype(o_ref.dtype)


def _gdn_step(gdn, ba, conv_buf, state0, conv_w, a_log, dt_bias, norm_w):
    b, gcols = gdn.shape
    nh = a_log.shape[0]
    hd = norm_w.shape[0]
    gw = nh * hd
    kw = conv_w.shape[0]
    arow = jnp.zeros((1, LANES), F32).at[0, nh:2 * nh].set(a_log)
    drow = jnp.zeros((1, LANES), F32).at[0, nh:2 * nh].set(dt_bias)
    kern = functools.partial(_gdn_step_kernel, nh=nh, hd=hd, kw=kw)
    const2 = lambda i: (0, 0)
    o, s = pl.pallas_call(
        kern,
        grid=(b,),
        in_specs=[
            pl.BlockSpec((1, 1, gcols), lambda i: (i, 0, 0)),
            pl.BlockSpec((1, 1, LANES), lambda i: (i, 0, 0)),
            pl.BlockSpec((1, kw - 1, 3 * gw), lambda i: (i, 0, 0)),
            pl.BlockSpec((1, nh, hd, hd), lambda i: (i, 0, 0, 0)),
            pl.BlockSpec((kw, 3 * gw), const2),
            pl.BlockSpec((1, LANES), const2),
            pl.BlockSpec((1, LANES), const2),
            pl.BlockSpec((1, hd), const2),
        ],
        out_specs=[
            pl.BlockSpec((1, 1, gw), lambda i: (i, 0, 0)),
            pl.BlockSpec((1, nh, hd, hd), lambda i: (i, 0, 0, 0)),
        ],
        out_shape=[jax.ShapeDtypeStruct((b, 1, gw), BF16), jax.ShapeDtypeStruct((b, nh, hd, hd), F32)],
        compiler_params=_cparams(("parallel",)),
        name="gdn_step",
    )(gdn.reshape(b, 1, gcols), ba.reshape(b, 1, LANES), conv_buf.astype(F32), state0.astype(F32),
      conv_w.astype(F32), arow, drow, norm_w.reshape(1, hd).astype(F32))
    return o.reshape(b, gw), s


def _sb_block_update(q, k, v, bias, carry, acc, mask, utri):
    g, tq, _ = q.shape
    tk = k.shape[1]
    zb = _bmm_nt(q, k)
    z = jnp.stack([zb[i] + bias[i] for i in range(g)])
    nlk = _softplus(z)
    if mask is not None:
        nlk = jnp.where(mask[None], nlk, 0.0)
    hi = nlk.astype(BF16)
    lo = (nlk - hi.astype(F32)).astype(BF16)
    hilo = jnp.concatenate([hi, lo], axis=2).reshape(g * tq, 2 * tk)
    from_here = jnp.dot(hilo, utri, preferred_element_type=F32).reshape(g, tq, tk)
    w = jnp.exp(z - from_here - carry)
    if mask is not None:
        w = jnp.where(mask[None], w, 0.0)
    acc = acc + _bmm(w, v)
    return carry + jnp.sum(nlk, axis=-1, keepdims=True), acc


def _sb_prompt_kernel(bias_ref, q_ref, k_ref, v_ref, o_ref, *, blk, hd, hp):
    qi = pl.program_id(2)
    hg = pl.program_id(1)
    bias = [bias_ref[hg * hp + i] for i in range(hp)]
    scale = hd ** -0.5
    q = jnp.stack([(q_ref[:, i * hd:(i + 1) * hd] * scale).astype(BF16) for i in range(hp)])
    ri = lax.broadcasted_iota(jnp.int32, (blk, blk), 0)
    ci = lax.broadcasted_iota(jnp.int32, (blk, blk), 1)
    utri = (ri >= ci).astype(BF16)
    utri = jnp.concatenate([utri, utri], axis=0)

    def heads(ref, start):
        return jnp.stack([ref[pl.ds(start, blk), i * hd:(i + 1) * hd] for i in range(hp)])

    start = pl.multiple_of(qi * blk, blk)
    carry, acc = _sb_block_update(q, heads(k_ref, start), heads(v_ref, start), bias,
                                  jnp.zeros((hp, blk, 1), F32), jnp.zeros((hp, blk, hd), F32), ci < ri, utri)

    def body(it, ca):
        s0 = pl.multiple_of((qi - 1 - it) * blk, blk)
        return _sb_block_update(q, heads(k_ref, s0), heads(v_ref, s0), bias, ca[0], ca[1], None, utri)

    carry, acc = lax.fori_loop(0, qi, body, (carry, acc))
    for i in range(hp):
        o_ref[:, i * hd:(i + 1) * hd] = acc[i].astype(o_ref.dtype)


def _sb_prompt(q, k, v, bias, batch, seq, hd):
    m, width = q.shape
    nh = width // hd
    hp = _tile(nh, SB_HEADS_PER_STEP)
    blk = _tile(seq, SB_BLOCK)
    nq = seq // blk
    kern = functools.partial(_sb_prompt_kernel, blk=blk, hd=hd, hp=hp)
    return pl.pallas_call(
        kern,
        grid=(batch, nh // hp, nq),
        in_specs=[
            pl.BlockSpec(memory_space=pltpu.SMEM),
            pl.BlockSpec((blk, hp * hd), lambda b, h, i: (b * nq + i, h)),
            pl.BlockSpec((seq, hp * hd), lambda b, h, i: (b, h)),
            pl.BlockSpec((seq, hp * hd), lambda b, h, i: (b, h)),
        ],
        out_specs=pl.BlockSpec((blk, hp * hd), lambda b, h, i: (b * nq + i, h)),
        out_shape=jax.ShapeDtypeStruct((m, width), BF16),
        compiler_params=_cparams(("parallel", "parallel", "arbitrary")),
        name="sb_prompt",
    )(bias.astype(F32), q, k, v)


def _paged_self_term(q, k_new, v_new, bcol, page, n_pages):
    nh, hd = q.shape
    z = jnp.sum(q * k_new, axis=-1, keepdims=True) * (hd ** -0.5) + bcol
    pos = jnp.full((nh, 1), n_pages * page, jnp.int32)
    visible = pos < pos
    nlk = jnp.where(visible, _softplus(z), 0.0)
    w = jnp.where(visible, jnp.exp(z - nlk), 0.0)
    return w * v_new, jnp.broadcast_to(nlk, (nh, LANES))


def _paged_pages(q, brow, k_pages, v_pages, run, acc):
    nh, hd = q.shape
    width = brow.shape[1]
    nt = width // LANES
    sub = lax.broadcasted_iota(jnp.int32, (nh, width), 0)
    lane = lax.broadcasted_iota(jnp.int32, (nh, width), 1)
    own = (lane % nh) == sub
    ri = lax.broadcasted_iota(jnp.int32, (LANES, 2 * LANES), 0)
    ci = lax.broadcasted_iota(jnp.int32, (LANES, 2 * LANES), 1)
    usum = ((ri > ci) | (ci >= LANES)).astype(BF16)
    qs = (q * (hd ** -0.5)).astype(BF16)
    z = [_mm_nt(qs, kp) + brow for kp in k_pages]
    nlk = [jnp.where(own, _softplus(zp), 0.0) for zp in z]
    tiles = jnp.concatenate([n[:, t * LANES:(t + 1) * LANES] for n in nlk for t in range(nt)], axis=0)
    hi = tiles.astype(BF16)
    lo = (tiles - hi.astype(F32)).astype(BF16)
    su = jnp.dot(hi, usum, preferred_element_type=F32) + jnp.dot(lo, usum, preferred_element_type=F32)
    for p, vp in enumerate(v_pages):
        later = []
        for t in reversed(range(nt)):
            r0 = (p * nt + t) * nh
            later.append(su[r0:r0 + nh, :LANES] + run)
            run = run + su[r0:r0 + nh, LANES:]
        later = jnp.concatenate(later[::-1], axis=1)
        w = jnp.where(own, jnp.exp(z[p] - nlk[p] - later), 0.0)
        acc = acc + _mm(w, vp)
    return run, acc


def _sb_paged_kernel(pt_ref, q_ref, knew_ref, vnew_ref, brow_ref, bcol_ref, *refs, page, npp, n_pages):
    k_refs, v_refs = refs[:npp], refs[npp:2 * npp]
    o_ref, carry_ref = refs[2 * npp], refs[2 * npp + 1]

    @pl.when(pl.program_id(1) == 0)
    def _():
        o_ref[0], carry_ref[...] = _paged_self_term(q_ref[0], knew_ref[0], vnew_ref[0], bcol_ref[...], page, n_pages)

    carry_ref[...], o_ref[0] = _paged_pages(q_ref[0], brow_ref[...], [r[...] for r in k_refs],
                                             [r[...] for r in v_refs], carry_ref[...], o_ref[0])


def _sb_paged(q, k_new, v_new, bias, cache_k, cache_v, layer, page_table):
    b, nh, hd = q.shape
    n_pool, page = cache_k.shape[1], cache_k.shape[2]
    n_pages = page_table.shape[1]
    npp = _tile(n_pages, PAGES_PER_STEP)
    width = page * nh
    ck = cache_k.reshape(cache_k.shape[0], n_pool, width, hd)
    cv = cache_v.reshape(cache_v.shape[0], n_pool, width, hd)
    brow = jnp.tile(bias.astype(F32), page).reshape(1, width)

    def page_spec(p):
        return pl.BlockSpec((None, None, width, hd),
                            lambda i, s, pt: (layer, pt[i * n_pages + n_pages - 1 - (s * npp + p)], 0, 0))

    kern = functools.partial(_sb_paged_kernel, page=page, npp=npp, n_pages=n_pages)
    head_spec = pl.BlockSpec((1, nh, hd), lambda i, s, pt: (i, 0, 0))
    grid_spec = pltpu.PrefetchScalarGridSpec(
        num_scalar_prefetch=1,
        grid=(b, n_pages // npp),
        in_specs=[head_spec, head_spec, head_spec, pl.BlockSpec((1, width), lambda i, s, pt: (0, 0)),
                  pl.BlockSpec((nh, 1), lambda i, s, pt: (0, 0))]
        + [page_spec(p) for p in range(npp)] + [page_spec(p) for p in range(npp)],
        out_specs=head_spec,
        scratch_shapes=[pltpu.VMEM((nh, LANES), F32)],
    )
    return pl.pallas_call(
        kern,
        grid_spec=grid_spec,
        out_shape=jax.ShapeDtypeStruct((b, nh, hd), F32),
        compiler_params=_cparams(("parallel", "arbitrary")),
        name="sb_paged",
    )(page_table.reshape(-1).astype(jnp.int32), q, k_new, v_new, brow, bias.astype(F32).reshape(nh, 1),
      *([ck] * npp), *([cv] * npp))


def _outproj_kernel(x_ref, ga_ref, a_ref, b_ref, wa_ref, wb_ref, o_ref):
    y = jnp.dot(a_ref[...], wa_ref[...], preferred_element_type=F32)
    y = y + jnp.dot(b_ref[...], wb_ref[...], preferred_element_type=F32)
    o_ref[...] = x_ref[...] + ga_ref[0] * y


def _outproj(x, mod3, rows_per_group, mix_a, mix_b, w_a, w_b):
    m, d = x.shape
    ka, kb = mix_a.shape[1], mix_b.shape[1]
    tm = _row_tile(m, mod3, rows_per_group, 512)
    tn = d
    return pl.pallas_call(
        _outproj_kernel,
        grid=(m // tm, d // tn),
        in_specs=[
            pl.BlockSpec((tm, tn), lambda i, j: (i, j)),
            _mod_spec(mod3, 2, d, tm, rows_per_group, tn),
            pl.BlockSpec((tm, ka), lambda i, j: (i, 0)),
            pl.BlockSpec((tm, kb), lambda i, j: (i, 0)),
            pl.BlockSpec((ka, tn), lambda i, j: (0, j)),
            pl.BlockSpec((kb, tn), lambda i, j: (0, j)),
        ],
        out_specs=pl.BlockSpec((tm, tn), lambda i, j: (i, j)),
        out_shape=jax.ShapeDtypeStruct((m, d), F32),
        compiler_params=_cparams(("parallel", "arbitrary")),
        name="out_proj",
    )(x, mod3, mix_a, mix_b, w_a, w_b)


def _page_copies(pt_ref, ck_ref, cv_ref, kbuf, vbuf, sem, step, slot, *, layer, ppg, n_pages):
    spb = n_pages // ppg
    b = lax.div(step, spb)
    s = lax.rem(step, spb)
    copies = []
    for p in range(ppg):
        phys = pt_ref[b * n_pages + (n_pages - 1 - (s * ppg + p))]
        copies.append(pltpu.make_async_copy(ck_ref.at[layer, phys], kbuf.at[slot, p], sem.at[slot]))
        copies.append(pltpu.make_async_copy(cv_ref.at[layer, phys], vbuf.at[slot, p], sem.at[slot]))
    return copies


def _ffn_kernel(*refs, final_norm, paged):
    if paged is None:
        x_ref, nw_ref, sh_ref, sc_ref, ga_ref, fw_ref, wg_ref, wu_ref, wd_ref, o_ref, h_scr, acc_scr = refs
    else:
        (pt_ref, x_ref, nw_ref, sh_ref, sc_ref, ga_ref, fw_ref, wg_ref, wu_ref, wd_ref, q_ref, knew_ref, vnew_ref,
         brow_ref, bcol_ref, ck_ref, cv_ref, o_ref, osb_ref, h_scr, acc_scr, kbuf, vbuf, carry_scr, sem) = refs
        page, ppg, n_pages = paged["page"], paged["ppg"], paged["n_pages"]
        spb = n_pages // ppg
        n_steps = paged["nb"] * spb
        copies = functools.partial(_page_copies, pt_ref, ck_ref, cv_ref, kbuf, vbuf, sem, layer=paged["layer"],
                                   ppg=ppg, n_pages=n_pages)
        t = pl.program_id(0) * pl.num_programs(1) + pl.program_id(1)

        @pl.when(t == 0)
        def _():
            for c in copies(0, 0):
                c.start()

        @pl.when(t + 1 < n_steps)
        def _():
            for c in copies(t + 1, lax.rem(t + 1, 2)):
                c.start()

    f = pl.program_id(1)

    @pl.when(f == 0)
    def _():
        h = _rms(x_ref[...]) * nw_ref[...]
        h_scr[...] = (h * (1.0 + sc_ref[0]) + sh_ref[0]).astype(BF16)
        acc_scr[...] = jnp.zeros_like(acc_scr)

    h = h_scr[...]
    g = jnp.dot(h, wg_ref[...], preferred_element_type=F32)
    u = jnp.dot(h, wu_ref[...], preferred_element_type=F32)
    acc_scr[...] += jnp.dot((_silu(g) * u).astype(BF16), wd_ref[...], preferred_element_type=F32)

    @pl.when(f == pl.num_programs(1) - 1)
    def _():
        y = x_ref[...] + ga_ref[0] * acc_scr[...]
        if final_norm:
            y = _rms(y) * fw_ref[...]
        o_ref[...] = y

    if paged is not None:
        @pl.when(t < n_steps)
        def _():
            slot = lax.rem(t, 2)
            for c in copies(t, slot):
                c.wait()
            b = lax.div(t, spb)

            @pl.when(lax.rem(t, spb) == 0)
            def _():
                osb_ref[b], carry_scr[...] = _paged_self_term(q_ref[b], knew_ref[b], vnew_ref[b], bcol_ref[...],
                                                              page, n_pages)

            carry_scr[...], osb_ref[b] = _paged_pages(
                q_ref[b], brow_ref[...], [kbuf[slot, p] for p in range(ppg)], [vbuf[slot, p] for p in range(ppg)],
                carry_scr[...], osb_ref[b])


def _ffn_tiles(m, mod3, rows_per_group, ff, hosts_pages):
    return _row_tile(m, mod3, rows_per_group, 512), _tile(ff, 256 if hosts_pages else 512)


def _ffn(x, nw, mod3, rows_per_group, fw, w_gate, w_up, w_down, final_norm, paged=None):
    m, d = x.shape
    ff = w_gate.shape[1]
    tm, tf = _ffn_tiles(m, mod3, rows_per_group, ff, paged is not None)
    specs = [
        pl.BlockSpec((tm, d), lambda i, f, *_: (i, 0)),
        pl.BlockSpec((1, d), lambda i, f, *_: (0, 0)),
        _mod_spec(mod3, 3, d, tm, rows_per_group),
        _mod_spec(mod3, 4, d, tm, rows_per_group),
        _mod_spec(mod3, 5, d, tm, rows_per_group),
        pl.BlockSpec((1, d), lambda i, f, *_: (0, 0)),
        pl.BlockSpec((d, tf), lambda i, f, *_: (0, f)),
        pl.BlockSpec((d, tf), lambda i, f, *_: (0, f)),
        pl.BlockSpec((tf, d), lambda i, f, *_: (f, 0)),
    ]
    args = [x, nw.reshape(1, d), mod3, mod3, mod3, fw.reshape(1, d), w_gate, w_up, w_down]
    out_spec = pl.BlockSpec((tm, d), lambda i, f, *_: (i, 0))
    out_shape = jax.ShapeDtypeStruct((m, d), F32)
    scratch = [pltpu.VMEM((tm, d), BF16), pltpu.VMEM((tm, d), F32)]
    grid = (m // tm, ff // tf)
    if paged is None:
        return pl.pallas_call(
            functools.partial(_ffn_kernel, final_norm=final_norm, paged=None),
            grid=grid, in_specs=specs, out_specs=out_spec, out_shape=out_shape, scratch_shapes=scratch,
            compiler_params=_cparams(("parallel", "arbitrary")), name="ffn",
        )(*args)

    q, k_new, v_new, bias, cache_k, cache_v, layer, page_table = paged
    nb, nh, hd = q.shape
    n_pool, page = cache_k.shape[1], cache_k.shape[2]
    n_pages = page_table.shape[1]
    ppg = _tile(n_pages, PAGES_PER_STEP)
    width = page * nh
    assert nb * (n_pages // ppg) <= grid[0] * grid[1]
    ck = cache_k.reshape(cache_k.shape[0], n_pool, width, hd)
    cv = cache_v.reshape(cache_v.shape[0], n_pool, width, hd)
    brow = jnp.tile(bias.astype(F32), page).reshape(1, width)
    whole = lambda shape: pl.BlockSpec(shape, lambda i, f, *_: (0,) * len(shape))
    info = dict(layer=layer, page=page, ppg=ppg, n_pages=n_pages, nb=nb)
    grid_spec = pltpu.PrefetchScalarGridSpec(
        num_scalar_prefetch=1,
        grid=grid,
        in_specs=specs + [whole((nb, nh, hd))] * 3 + [whole((1, width)), whole((nh, 1)),
                                                       pl.BlockSpec(memory_space=pl.ANY),
                                                       pl.BlockSpec(memory_space=pl.ANY)],
        out_specs=[out_spec, whole((nb, nh, hd))],
        scratch_shapes=scratch + [pltpu.VMEM((2, ppg, width, hd), F32), pltpu.VMEM((2, ppg, width, hd), F32),
                                  pltpu.VMEM((nh, LANES), F32), pltpu.SemaphoreType.DMA((2,))],
    )
    return pl.pallas_call(
        functools.partial(_ffn_kernel, final_norm=final_norm, paged=info),
        grid_spec=grid_spec,
        out_shape=[out_shape, jax.ShapeDtypeStruct((nb, nh, hd), F32)],
        compiler_params=_cparams(("arbitrary", "arbitrary")),
        name="ffn_paged",
    )(page_table.reshape(-1).astype(jnp.int32), *args, q, k_new, v_new, brow, bias.astype(F32).reshape(nh, 1), ck, cv)


def kernel(x_prompt, x_sample, c_prompt, c_sample, cache_k, cache_v, state_gdn, state_conv, page_table, norm1_w, norm2_w, w_ada, b_ada, w_in, conv_w, a_log, dt_bias, gdn_norm_w, sb_bias, w_out, w_gate, w_up, w_down, final_norm_w):
    depth = w_in.shape[0]
    bp, seq, d = x_prompt.shape
    bs, seq_s, _ = x_sample.shape
    assert seq_s == 1, "the sample group advances one token per step"
    nh_g, nh_s = a_log.shape[1], sb_bias.shape[1]
    hd = gdn_norm_w.shape[1]
    gw, sw = nh_g * hd, nh_s * hd
    kw = conv_w.shape[1]
    c_ba = 4 * gw
    c_sb = c_ba + 2 * nh_g
    n_bat = -(-2 * nh_g // SUBLANES) * SUBLANES

    hp = x_prompt.reshape(bp * seq, d)
    hs = x_sample.reshape(bs, d)
    outs = {n: [] for n in ("kp", "vp", "gp", "cp", "ks", "vs", "gs", "cs")}
    for l in range(depth):
        w_main = jnp.concatenate([w_in[l][:, :c_ba], w_in[l][:, c_sb:]], axis=1).astype(BF16)
        w_ba = jnp.zeros((d, LANES), BF16).at[:, :2 * nh_g].set(w_in[l][:, c_ba:c_sb].astype(BF16))
        wo_a, wo_b = w_out[l][:gw].astype(BF16), w_out[l][gw:].astype(BF16)
        wg, wu, wd = w_gate[l].astype(BF16), w_up[l].astype(BF16), w_down[l].astype(BF16)
        last = l == depth - 1

        mod_p, mod_s = _ada(c_prompt, c_sample, w_ada[l], b_ada[l])
        mod_p3 = mod_p.reshape(bp, 1, 6 * d)
        mod_s3 = mod_s.reshape(1, bs, 6 * d)

        g_s, q_s, k_s, v_s, ba_s = _inproj(hs, norm1_w[l], mod_s3, 1, w_main, w_ba, 4 * gw, sw, 0)
        paged = (q_s.reshape(bs, nh_s, hd), k_s.reshape(bs, nh_s, hd), v_s.reshape(bs, nh_s, hd), sb_bias[l],
                 cache_k, cache_v, l, page_table)
        tm_p, tf_p = _ffn_tiles(bp * seq, mod_p3, seq, wg.shape[1], True)
        n_pages = page_table.shape[1]
        hosted = bs * (n_pages // _tile(n_pages, PAGES_PER_STEP)) <= (bp * seq // tm_p) * (wg.shape[1] // tf_p)

        g_p, q_p, k_p, v_p, ba_p, bat_p = _inproj(hp, norm1_w[l], mod_p3, seq, w_main, w_ba, 4 * gw, sw, n_bat)
        gdn_o, gdn_s, conv_o = _gdn_chunked(
            g_p, ba_p, bat_p, jnp.zeros((bp, kw - 1, 3 * gw), F32), jnp.zeros((bp, nh_g, hd, hd), F32),
            conv_w[l], a_log[l], dt_bias[l], gdn_norm_w[l], bp, seq)
        sb_o = _sb_prompt(q_p, k_p, v_p, sb_bias[l], bp, seq, hd)
        hp = _outproj(hp, mod_p3, seq, gdn_o, sb_o, wo_a, wo_b)
        if hosted:
            hp, sb_s = _ffn(hp, norm2_w[l], mod_p3, seq, final_norm_w, wg, wu, wd, last, paged)
        else:
            hp = _ffn(hp, norm2_w[l], mod_p3, seq, final_norm_w, wg, wu, wd, last)
            sb_s = _sb_paged(*paged)
        outs["kp"].append(k_p.reshape(bp, seq, nh_s, hd))
        outs["vp"].append(v_p.reshape(bp, seq, nh_s, hd))
        outs["gp"].append(gdn_s)
        outs["cp"].append(conv_o)

        gdn_o, gdn_s = _gdn_step(g_s, ba_s, state_conv[l], state_gdn[l], conv_w[l], a_log[l], dt_bias[l],
                                 gdn_norm_w[l])
        hs = _outproj(hs, mod_s3, 1, gdn_o, sb_s.reshape(bs, sw).astype(BF16), wo_a, wo_b)
        hs = _ffn(hs, norm2_w[l], mod_s3, 1, final_norm_w, wg, wu, wd, last)
        outs["ks"].append(k_s.reshape(bs, 1, nh_s, hd))
        outs["vs"].append(v_s.reshape(bs, 1, nh_s, hd))
        outs["gs"].append(gdn_s.astype(state_gdn.dtype))
        outs["cs"].append(jnp.concatenate([state_conv[l][:, 1:], g_s[:, None, :3 * gw]], axis=1).astype(state_conv.dtype))

    st = lambda n: jnp.stack(outs[n])
    return (hp.reshape(bp, seq, d), hs.reshape(bs, 1, d), st("kp"), st("vp"), st("gp"), st("cp"),
            st("ks"), st("vs"), st("gs"), st("cs"))
```

```python
import functools
import math

import jax
import jax.numpy as jnp
from jax import lax
from jax.experimental import pallas as pl
from jax.experimental.pallas import tpu as pltpu

F32 = jnp.float32
BF16 = jnp.bfloat16
NORM_EPS = 1e-6
LOG2E = 1.4426950408889634
LANES = 128
SUBLANES = 8
GDN_CHUNK = 128
INV_BLOCK = 16
SB_BLOCK = 256
SB_HEADS_PER_STEP = 8
PAGES_PER_STEP = 8
VMEM_LIMIT = 56 * 1024 * 1024


def _cparams(sem):
    return pltpu.CompilerParams(dimension_semantics=sem, vmem_limit_bytes=VMEM_LIMIT)


def _tile(n, pref):
    if n <= pref:
        return n
    t = pref
    while n % t:
        t //= 2
    return t


def _mm(a, b):
    return jnp.dot(a.astype(BF16), b.astype(BF16), preferred_element_type=F32)


def _mm_nt(a, b):
    return lax.dot_general(a.astype(BF16), b.astype(BF16), (((1,), (1,)), ((), ())), preferred_element_type=F32)


def _mm_tn(a, b):
    return lax.dot_general(a.astype(BF16), b.astype(BF16), (((0,), (0,)), ((), ())), preferred_element_type=F32)


def _split3(x):
    hi = x.astype(BF16)
    r = x - hi.astype(F32)
    mid = r.astype(BF16)
    lo = (r - mid.astype(F32)).astype(BF16)
    return hi, mid, lo


def _dot_01_rhs(x, m01):
    return sum(jnp.dot(p, m01, preferred_element_type=F32) for p in _split3(x))


def _dot_01_lhs(m01, x):
    return sum(jnp.dot(m01, p, preferred_element_type=F32) for p in _split3(x))


def _softplus(x):
    return jnp.maximum(x, 0.0) + jnp.log(1.0 + jnp.exp2(jnp.abs(x) * (-LOG2E)))


def _bmm(a, b):
    return lax.dot_general(a.astype(BF16), b.astype(BF16), (((2,), (1,)), ((0,), (0,))), preferred_element_type=F32)


def _bmm_nt(a, b):
    return lax.dot_general(a.astype(BF16), b.astype(BF16), (((2,), (2,)), ((0,), (0,))), preferred_element_type=F32)


def _bmm_tn(a, b):
    return lax.dot_general(a.astype(BF16), b.astype(BF16), (((1,), (1,)), ((0,), (0,))), preferred_element_type=F32)


def _silu(x):
    return x * jax.nn.sigmoid(x)


def _rms(x):
    return x * lax.rsqrt(jnp.mean(x * x, axis=-1, keepdims=True) + NORM_EPS)


def _ada_kernel(cp_ref, cs_ref, w_ref, b_ref, op_ref, os_ref):
    w = w_ref[...].astype(BF16)
    b = b_ref[...]
    for c_ref, o_ref in ((cp_ref, op_ref), (cs_ref, os_ref)):
        o_ref[...] = jnp.dot(_silu(c_ref[...]).astype(BF16), w, preferred_element_type=F32) + b


def _ada(c_p, c_s, w, b):
    d, n = w.shape
    tn = _tile(n, 1024)
    bp, bs = c_p.shape[0], c_s.shape[0]
    return pl.pallas_call(
        _ada_kernel,
        grid=(n // tn,),
        in_specs=[
            pl.BlockSpec((bp, d), lambda j: (0, 0)),
            pl.BlockSpec((bs, d), lambda j: (0, 0)),
            pl.BlockSpec((d, tn), lambda j: (0, j)),
            pl.BlockSpec((1, tn), lambda j: (0, j)),
        ],
        out_specs=[pl.BlockSpec((bp, tn), lambda j: (0, j)), pl.BlockSpec((bs, tn), lambda j: (0, j))],
        out_shape=[jax.ShapeDtypeStruct((bp, n), F32), jax.ShapeDtypeStruct((bs, n), F32)],
        compiler_params=_cparams(("parallel",)),
        name="ada_mod",
    )(c_p, c_s, w, b.reshape(1, n))


def _row_tile(m, mod3, rows_per_group, pref):
    return m if mod3.shape[1] != 1 else _tile(rows_per_group, pref)


def _mod_spec(mod3, comp, d, tm, rows_per_group, tn=None):
    per_row = mod3.shape[1] != 1
    assert not per_row or mod3.shape[1] == tm
    if tn is None:
        col = lambda j: comp
        tn = d
    else:
        col = lambda j: comp * (d // tn) + j
    if per_row:
        return pl.BlockSpec((1, tm, tn), lambda i, j, *_: (0, 0, col(j)))
    return pl.BlockSpec((1, 1, tn), lambda i, j, *_: (i * tm // rows_per_group, 0, col(j)))


def _inproj_kernel(x_ref, nw_ref, sc_ref, sh_ref, w_ref, wba_ref, og_ref, oq_ref, ok_ref, ov_ref, oba_ref, *rest,
                   n_g, n_h, n_bat):
    obat_ref = rest[0] if n_bat else None
    h_scr = rest[-1]
    j = pl.program_id(1)

    @pl.when(j == 0)
    def _():
        h = _rms(x_ref[...]) * nw_ref[...]
        h = (h * (1.0 + sc_ref[0]) + sh_ref[0]).astype(BF16)
        h_scr[...] = h
        ba = jnp.dot(h, wba_ref[...], preferred_element_type=F32)
        oba_ref[...] = ba
        if n_bat:
            obat_ref[...] = ba.T[:n_bat]

    acc = jnp.dot(h_scr[...], w_ref[...], preferred_element_type=F32)
    for ref, lo, hi in ((og_ref, 0, n_g), (oq_ref, n_g, n_g + n_h), (ok_ref, n_g + n_h, n_g + 2 * n_h),
                        (ov_ref, n_g + 2 * n_h, n_g + 3 * n_h)):
        @pl.when((j >= lo) & (j < hi))
        def _(ref=ref):
            ref[...] = acc


def _inproj(x, nw, mod3, rows_per_group, w_main, w_ba, gdn_cols, sb_cols, n_bat):
    m, d = x.shape
    tm = _row_tile(m, mod3, rows_per_group, 1024)
    tn = _tile(math.gcd(gdn_cols, sb_cols), 512)
    n_g, n_h = gdn_cols // tn, sb_cols // tn
    nj = n_g + 3 * n_h

    def seg(lo, n):
        return lambda i, j: (i, jnp.clip(j - lo, 0, n - 1))

    kern = functools.partial(_inproj_kernel, n_g=n_g, n_h=n_h, n_bat=n_bat)
    return pl.pallas_call(
        kern,
        grid=(m // tm, nj),
        in_specs=[
            pl.BlockSpec((tm, d), lambda i, j: (i, 0)),
            pl.BlockSpec((1, d), lambda i, j: (0, 0)),
            _mod_spec(mod3, 1, d, tm, rows_per_group),
            _mod_spec(mod3, 0, d, tm, rows_per_group),
            pl.BlockSpec((d, tn), lambda i, j: (0, j)),
            pl.BlockSpec((d, LANES), lambda i, j: (0, 0)),
        ],
        out_specs=[
            pl.BlockSpec((tm, tn), seg(0, n_g)),
            pl.BlockSpec((tm, tn), seg(n_g, n_h)),
            pl.BlockSpec((tm, tn), seg(n_g + n_h, n_h)),
            pl.BlockSpec((tm, tn), seg(n_g + 2 * n_h, n_h)),
            pl.BlockSpec((tm, LANES), lambda i, j: (i, 0)),
        ] + ([pl.BlockSpec((n_bat, tm), lambda i, j: (0, i))] if n_bat else []),
        out_shape=[
            jax.ShapeDtypeStruct((m, gdn_cols), F32),
            jax.ShapeDtypeStruct((m, sb_cols), F32),
            jax.ShapeDtypeStruct((m, sb_cols), F32),
            jax.ShapeDtypeStruct((m, sb_cols), F32),
            jax.ShapeDtypeStruct((m, LANES), F32),
        ] + ([jax.ShapeDtypeStruct((n_bat, m), F32)] if n_bat else []),
        scratch_shapes=[pltpu.VMEM((tm, d), BF16)],
        compiler_params=_cparams(("parallel", "arbitrary")),
        name="in_proj",
    )(x, nw.reshape(1, d), mod3, mod3, w_main, w_ba)


def _unit_lower_inverse_minus_eye(a, ri, ci):
    c = a.shape[-1]
    blk = ((ri // INV_BLOCK) == (ci // INV_BLOCK))[None]
    ad = jnp.where(blk, a, 0.0)
    off = a - ad
    xn = -ad
    p = ad
    for _ in range(int(math.log2(INV_BLOCK)) - 1):
        p = _bmm(p, p)
        xn = xn + p + _bmm(xn, p)
    m = off + _bmm(xn, off)
    yn = -m
    p = m
    for _ in range(int(math.log2(c // INV_BLOCK)) - 1):
        p = _bmm(p, p)
        yn = yn + p + _bmm(yn, p)
    return yn + xn + _bmm(yn, xn)


def _gdn_chunk_kernel(g_ref, ba_ref, bat_ref, cb_ref, s0_ref, cw_ref, arow_ref, drow_ref, acol_ref, dcol_ref, nw_ref,
                      o_ref, s_ref, hist_ref, *, nh, hd, chunk, kw):
    c = pl.program_id(1)

    @pl.when(c == 0)
    def _():
        s_ref[...] = s0_ref[...]
        hist_ref[...] = cb_ref[...]

    gw = nh * hd
    ri = lax.broadcasted_iota(jnp.int32, (chunk, chunk), 0)
    ci = lax.broadcasted_iota(jnp.int32, (chunk, chunk), 1)
    incl = ci <= ri
    strict = ci < ri
    ltri = incl.astype(BF16)
    utri = (ri <= ci).astype(BF16)

    ba = ba_ref[...]
    bat = bat_ref[...]
    beta_all = jax.nn.sigmoid(ba)
    g_all = -jnp.exp(arow_ref[...]) * _softplus(ba + drow_ref[...])
    g_t = -jnp.exp(acol_ref[...]) * _softplus(bat + dcol_ref[...])
    gc_all = _dot_01_lhs(ltri, g_all)
    gc_t = _dot_01_rhs(g_t, utri)
    scale = hd ** -0.5

    hist = hist_ref[0]
    x_all = g_ref[...]
    cw = cw_ref[...]
    nw = nw_ref[...]
    pad = hist.shape[0]

    xw = jnp.concatenate([hist, x_all[:, :3 * gw]], axis=0)
    y = cw[kw - 1:kw] * xw[pad:]
    for i in range(kw - 1):
        sft = kw - 1 - i
        y = y + cw[i:i + 1] * xw[pad - sft:pad - sft + chunk]
    y = _silu(y)

    def heads(arr, base):
        return jnp.stack([arr[:, base + h * hd:base + (h + 1) * hd] for h in range(nh)])

    def l2n(t):
        return t * lax.rsqrt(jnp.sum(t * t, axis=-1, keepdims=True) + NORM_EPS)

    q = l2n(heads(y, 0)) * scale
    k = l2n(heads(y, gw))
    v = heads(y, 2 * gw)
    beta = jnp.stack([beta_all[:, h:h + 1] for h in range(nh)])
    gc = jnp.stack([gc_all[:, nh + h:nh + h + 1] for h in range(nh)])
    gc_row = jnp.stack([gc_t[nh + h:nh + h + 1, :] for h in range(nh)])
    gl = gc[:, chunk - 1:chunk, :]
    decay = jnp.exp(jnp.where(incl[None], gc - gc_row, -1e30))
    kb = k * beta
    a = jnp.where(strict[None], _bmm_nt(kb, k) * decay, 0.0)
    tn = _unit_lower_inverse_minus_eye(a, ri, ci)
    egc = jnp.exp(gc)
    r = jnp.concatenate([v * beta, kb * egc], axis=2)
    uw = r + _bmm(tn, r)
    u, w = uw[:, :, :hd], uw[:, :, hd:]
    qk = jnp.where(incl[None], _bmm_nt(q, k) * decay, 0.0)
    s = s_ref[0]
    v_new = u - _bmm(w, s)
    o = _bmm(q * egc, s) + _bmm(qk, v_new)
    s_ref[0] = s * jnp.exp(gl) + _bmm_tn(k * jnp.exp(gl - gc), v_new)
    out = (_rms(o) * nw * _silu(heads(x_all, 3 * gw))).astype(o_ref.dtype)
    for h in range(nh):
        o_ref[:, h * hd:(h + 1) * hd] = out[h]

    hist_ref[0] = x_all[chunk - pad:, :3 * gw]


def _gdn_chunked(gdn, ba, bat, conv_buf, state0, conv_w, a_log, dt_bias, norm_w, batch, seq):
    m, gcols = gdn.shape
    nh = a_log.shape[0]
    hd = norm_w.shape[0]
    gw = nh * hd
    kw = conv_w.shape[0]
    chunk = _tile(seq, GDN_CHUNK)
    nc = seq // chunk
    n_bat = bat.shape[0]
    pad = SUBLANES
    cb = jnp.zeros((batch, pad, 3 * gw), F32).at[:, pad - (kw - 1):].set(conv_buf.astype(F32))
    cw = jnp.zeros((pad, 3 * gw), F32).at[:kw].set(conv_w.astype(F32))
    arow = jnp.zeros((1, LANES), F32).at[0, nh:2 * nh].set(a_log)
    drow = jnp.zeros((1, LANES), F32).at[0, nh:2 * nh].set(dt_bias)
    acol = jnp.zeros((n_bat, 1), F32).at[nh:2 * nh, 0].set(a_log)
    dcol = jnp.zeros((n_bat, 1), F32).at[nh:2 * nh, 0].set(dt_bias)
    kern = functools.partial(_gdn_chunk_kernel, nh=nh, hd=hd, chunk=chunk, kw=kw)
    const2 = lambda b, c: (0, 0)
    o, s, hist = pl.pallas_call(
        kern,
        grid=(batch, nc),
        in_specs=[
            pl.BlockSpec((chunk, gcols), lambda b, c: (b * nc + c, 0)),
            pl.BlockSpec((chunk, LANES), lambda b, c: (b * nc + c, 0)),
            pl.BlockSpec((n_bat, chunk), lambda b, c: (0, b * nc + c)),
            pl.BlockSpec((1, pad, 3 * gw), lambda b, c: (b, 0, 0)),
            pl.BlockSpec((1, nh, hd, hd), lambda b, c: (b, 0, 0, 0)),
            pl.BlockSpec((pad, 3 * gw), const2),
            pl.BlockSpec((1, LANES), const2),
            pl.BlockSpec((1, LANES), const2),
            pl.BlockSpec((n_bat, 1), const2),
            pl.BlockSpec((n_bat, 1), const2),
            pl.BlockSpec((1, hd), const2),
        ],
        out_specs=[
            pl.BlockSpec((chunk, gw), lambda b, c: (b * nc + c, 0)),
            pl.BlockSpec((1, nh, hd, hd), lambda b, c: (b, 0, 0, 0)),
            pl.BlockSpec((1, pad, 3 * gw), lambda b, c: (b, 0, 0)),
        ],
        out_shape=[
            jax.ShapeDtypeStruct((m, gw), BF16),
            jax.ShapeDtypeStruct((batch, nh, hd, hd), F32),
            jax.ShapeDtypeStruct((batch, pad, 3 * gw), F32),
        ],
        compiler_params=_cparams(("parallel", "arbitrary")),
        name="gdn_chunked",
    )(gdn, ba, bat, cb, state0.astype(F32), cw, arow, drow, acol, dcol, norm_w.reshape(1, hd).astype(F32))
    return o, s, hist[:, pad - (kw - 1):]


def _gdn_step_kernel(g_ref, ba_ref, cb_ref, s0_ref, cw_ref, arow_ref, drow_ref, nw_ref, o_ref, s_ref, *, nh, hd, kw):
    gw = nh * hd
    x = g_ref[0]
    cb = cb_ref[0]
    cw = cw_ref[...]
    y = cw[kw - 1:kw] * x[:, :3 * gw]
    for i in range(kw - 1):
        y = y + cw[i:i + 1] * cb[i:i + 1]
    y = _silu(y)
    ba = ba_ref[0]
    beta_all = jax.nn.sigmoid(ba)
    g_all = -jnp.exp(arow_ref[...]) * _softplus(ba + drow_ref[...])
    nw = nw_ref[...]
    row = lax.broadcasted_iota(jnp.int32, (SUBLANES, hd), 0)
    scale = hd ** -0.5

    def l2n(t):
        return t * lax.rsqrt(jnp.sum(t * t, axis=-1, keepdims=True) + NORM_EPS)

    for h in range(nh):
        q = l2n(y[:, h * hd:(h + 1) * hd]) * scale
        k = l2n(y[:, gw + h * hd:gw + (h + 1) * hd])
        v = y[:, 2 * gw + h * hd:2 * gw + (h + 1) * hd]
        beta = beta_all[:, h:h + 1]
        eg = jnp.exp(g_all[:, nh + h:nh + h + 1])
        s = s0_ref[0, h]
        lhs = jnp.where(row == 0, jnp.broadcast_to(k, (SUBLANES, hd)),
                        jnp.where(row == 1, jnp.broadcast_to(q, (SUBLANES, hd)), 0.0))
        ks_qs = _mm(lhs, s)
        ks, qs = ks_qs[0:1], ks_qs[1:2]
        v_new = beta * v - (beta * eg) * ks
        qk = jnp.sum(q * k, axis=-1, keepdims=True)
        o = eg * qs + qk * v_new
        k8 = jnp.where(row == 0, jnp.broadcast_to(k, (SUBLANES, hd)), 0.0)
        v8 = jnp.where(row == 0, jnp.broadcast_to(v_new, (SUBLANES, hd)), 0.0)
        s_ref[0, h] = s * eg + _mm_tn(k8, v8)
        z = x[:, 3 * gw + h * hd:3 * gw + (h + 1) * hd]
        o_ref[0, :, h * hd:(h + 1) * hd] = (_rms(o) * nw * _silu(z)).astype(o_ref.dtype)


def _gdn_step(gdn, ba, conv_buf, state0, conv_w, a_log, dt_bias, norm_w):
    b, gcols = gdn.shape
    nh = a_log.shape[0]
    hd = norm_w.shape[0]
    gw = nh * hd
    kw = conv_w.shape[0]
    arow = jnp.zeros((1, LANES), F32).at[0, nh:2 * nh].set(a_log)
    drow = jnp.zeros((1, LANES), F32).at[0, nh:2 * nh].set(dt_bias)
    kern = functools.partial(_gdn_step_kernel, nh=nh, hd=hd, kw=kw)
    const2 = lambda i: (0, 0)
    o, s = pl.pallas_call(
        kern,
        grid=(b,),
        in_specs=[
            pl.BlockSpec((1, 1, gcols), lambda i: (i, 0, 0)),
            pl.BlockSpec((1, 1, LANES), lambda i: (i, 0, 0)),
            pl.BlockSpec((1, kw - 1, 3 * gw), lambda i: (i, 0, 0)),
            pl.BlockSpec((1, nh, hd, hd), lambda i: (i, 0, 0, 0)),
            pl.BlockSpec((kw, 3 * gw), const2),
            pl.BlockSpec((1, LANES), const2),
            pl.BlockSpec((1, LANES), const2),
            pl.BlockSpec((1, hd), const2),
        ],
        out_specs=[
            pl.BlockSpec((1, 1, gw), lambda i: (i, 0, 0)),
            pl.BlockSpec((1, nh, hd, hd), lambda i: (i, 0, 0, 0)),
        ],
        out_shape=[jax.ShapeDtypeStruct((b, 1, gw), BF16), jax.ShapeDtypeStruct((b, nh, hd, hd), F32)],
        compiler_params=_cparams(("parallel",)),
        name="gdn_step",
    )(gdn.reshape(b, 1, gcols), ba.reshape(b, 1, LANES), conv_buf.astype(F32), state0.astype(F32),
      conv_w.astype(F32), arow, drow, norm_w.reshape(1, hd).astype(F32))
    return o.reshape(b, gw), s


def _sb_block_update(q, k, v, bias, carry, acc, mask, utri):
    g, tq, _ = q.shape
    tk = k.shape[1]
    zb = _bmm_nt(q, k)
    z = jnp.stack([zb[i] + bias[i] for i in range(g)])
    nlk = _softplus(z)
    if mask is not None:
        nlk = jnp.where(mask[None], nlk, 0.0)
    hi = nlk.astype(BF16)
    lo = (nlk - hi.astype(F32)).astype(BF16)
    hilo = jnp.concatenate([hi, lo], axis=2).reshape(g * tq, 2 * tk)
    from_here = jnp.dot(hilo, utri, preferred_element_type=F32).reshape(g, tq, tk)
    w = jnp.exp(z - from_here - carry)
    if mask is not None:
        w = jnp.where(mask[None], w, 0.0)
    acc = acc + _bmm(w, v)
    return carry + jnp.sum(nlk, axis=-1, keepdims=True), acc


def _sb_prompt_kernel(bias_ref, q_ref, k_ref, v_ref, o_ref, *, blk, hd, hp):
    qi = pl.program_id(2)
    hg = pl.program_id(1)
    bias = [bias_ref[hg * hp + i] for i in range(hp)]
    scale = hd ** -0.5
    q = jnp.stack([(q_ref[:, i * hd:(i + 1) * hd] * scale).astype(BF16) for i in range(hp)])
    ri = lax.broadcasted_iota(jnp.int32, (blk, blk), 0)
    ci = lax.broadcasted_iota(jnp.int32, (blk, blk), 1)
    utri = (ri >= ci).astype(BF16)
    utri = jnp.concatenate([utri, utri], axis=0)

    def heads(ref, start):
        return jnp.stack([ref[pl.ds(start, blk), i * hd:(i + 1) * hd] for i in range(hp)])

    start = pl.multiple_of(qi * blk, blk)
    carry, acc = _sb_block_update(q, heads(k_ref, start), heads(v_ref, start), bias,
                                  jnp.zeros((hp, blk, 1), F32), jnp.zeros((hp, blk, hd), F32), ci < ri, utri)

    def body(it, ca):
        s0 = pl.multiple_of((qi - 1 - it) * blk, blk)
        return _sb_block_update(q, heads(k_ref, s0), heads(v_ref, s0), bias, ca[0], ca[1], None, utri)

    carry, acc = lax.fori_loop(0, qi, body, (carry, acc))
    for i in range(hp):
        o_ref[:, i * hd:(i + 1) * hd] = acc[i].astype(o_ref.dtype)


def _sb_prompt(q, k, v, bias, batch, seq, hd):
    m, width = q.shape
    nh = width // hd
    hp = _tile(nh, SB_HEADS_PER_STEP)
    blk = _tile(seq, SB_BLOCK)
    nq = seq // blk
    kern = functools.partial(_sb_prompt_kernel, blk=blk, hd=hd, hp=hp)
    return pl.pallas_call(
        kern,
        grid=(batch, nh // hp, nq),
        in_specs=[
            pl.BlockSpec(memory_space=pltpu.SMEM),
            pl.BlockSpec((blk, hp * hd), lambda b, h, i: (b * nq + i, h)),
            pl.BlockSpec((seq, hp * hd), lambda b, h, i: (b, h)),
            pl.BlockSpec((seq, hp * hd), lambda b, h, i: (b, h)),
        ],
        out_specs=pl.BlockSpec((blk, hp * hd), lambda b, h, i: (b * nq + i, h)),
        out_shape=jax.ShapeDtypeStruct((m, width), BF16),
        compiler_params=_cparams(("parallel", "parallel", "arbitrary")),
        name="sb_prompt",
    )(bias.astype(F32), q, k, v)


def _paged_self_term(q, k_new, v_new, bcol, page, n_pages):
    nh, hd = q.shape
    z = jnp.sum(q * k_new, axis=-1, keepdims=True) * (hd ** -0.5) + bcol
    pos = jnp.full((nh, 1), n_pages * page, jnp.int32)
    visible = pos < pos
    nlk = jnp.where(visible, _softplus(z), 0.0)
    w = jnp.where(visible, jnp.exp(z - nlk), 0.0)
    return w * v_new, jnp.broadcast_to(nlk, (nh, LANES))


def _paged_pages(q, brow, k_pages, v_pages, run, acc):
    nh, hd = q.shape
    width = brow.shape[1]
    nt = width // LANES
    sub = lax.broadcasted_iota(jnp.int32, (nh, width), 0)
    lane = lax.broadcasted_iota(jnp.int32, (nh, width), 1)
    own = (lane % nh) == sub
    ri = lax.broadcasted_iota(jnp.int32, (LANES, 2 * LANES), 0)
    ci = lax.broadcasted_iota(jnp.int32, (LANES, 2 * LANES), 1)
    usum = ((ri > ci) | (ci >= LANES)).astype(BF16)
    qs = (q * (hd ** -0.5)).astype(BF16)
    z = [_mm_nt(qs, kp) + brow for kp in k_pages]
    nlk = [jnp.where(own, _softplus(zp), 0.0) for zp in z]
    tiles = jnp.concatenate([n[:, t * LANES:(t + 1) * LANES] for n in nlk for t in range(nt)], axis=0)
    hi = tiles.astype(BF16)
    lo = (tiles - hi.astype(F32)).astype(BF16)
    su = jnp.dot(hi, usum, preferred_element_type=F32) + jnp.dot(lo, usum, preferred_element_type=F32)
    for p, vp in enumerate(v_pages):
        later = []
        for t in reversed(range(nt)):
            r0 = (p * nt + t) * nh
            later.append(su[r0:r0 + nh, :LANES] + run)
            run = run + su[r0:r0 + nh, LANES:]
        later = jnp.concatenate(later[::-1], axis=1)
        w = jnp.where(own, jnp.exp(z[p] - nlk[p] - later), 0.0)
        acc = acc + _mm(w, vp)
    return run, acc


def _sb_paged_kernel(pt_ref, q_ref, knew_ref, vnew_ref, brow_ref, bcol_ref, *refs, page, npp, n_pages):
    k_refs, v_refs = refs[:npp], refs[npp:2 * npp]
    o_ref, carry_ref = refs[2 * npp], refs[2 * npp + 1]

    @pl.when(pl.program_id(1) == 0)
    def _():
        o_ref[0], carry_ref[...] = _paged_self_term(q_ref[0], knew_ref[0], vnew_ref[0], bcol_ref[...], page, n_pages)

    carry_ref[...], o_ref[0] = _paged_pages(q_ref[0], brow_ref[...], [r[...] for r in k_refs],
                                             [r[...] for r in v_refs], carry_ref[...], o_ref[0])


def _sb_paged(q, k_new, v_new, bias, cache_k, cache_v, layer, page_table):
    b, nh, hd = q.shape
    n_pool, page = cache_k.shape[1], cache_k.shape[2]
    n_pages = page_table.shape[1]
    npp = _tile(n_pages, PAGES_PER_STEP)
    width = page * nh
    ck = cache_k.reshape(cache_k.shape[0], n_pool, width, hd)
    cv = cache_v.reshape(cache_v.shape[0], n_pool, width, hd)
    brow = jnp.tile(bias.astype(F32), page).reshape(1, width)

    def page_spec(p):
        return pl.BlockSpec((None, None, width, hd),
                            lambda i, s, pt: (layer, pt[i * n_pages + n_pages - 1 - (s * npp + p)], 0, 0))

    kern = functools.partial(_sb_paged_kernel, page=page, npp=npp, n_pages=n_pages)
    head_spec = pl.BlockSpec((1, nh, hd), lambda i, s, pt: (i, 0, 0))
    grid_spec = pltpu.PrefetchScalarGridSpec(
        num_scalar_prefetch=1,
        grid=(b, n_pages // npp),
        in_specs=[head_spec, head_spec, head_spec, pl.BlockSpec((1, width), lambda i, s, pt: (0, 0)),
                  pl.BlockSpec((nh, 1), lambda i, s, pt: (0, 0))]
        + [page_spec(p) for p in range(npp)] + [page_spec(p) for p in range(npp)],
        out_specs=head_spec,
        scratch_shapes=[pltpu.VMEM((nh, LANES), F32)],
    )
    return pl.pallas_call(
        kern,
        grid_spec=grid_spec,
        out_shape=jax.ShapeDtypeStruct((b, nh, hd), F32),
        compiler_params=_cparams(("parallel", "arbitrary")),
        name="sb_paged",
    )(page_table.reshape(-1).astype(jnp.int32), q, k_new, v_new, brow, bias.astype(F32).reshape(nh, 1),
      *([ck] * npp), *([cv] * npp))


def _outproj_kernel(x_ref, ga_ref, a_ref, b_ref, wa_ref, wb_ref, o_ref):
    y = jnp.dot(a_ref[...], wa_ref[...], preferred_element_type=F32)
    y = y + jnp.dot(b_ref[...], wb_ref[...], preferred_element_type=F32)
    o_ref[...] = x_ref[...] + ga_ref[0] * y


def _outproj(x, mod3, rows_per_group, mix_a, mix_b, w_a, w_b):
    m, d = x.shape
    ka, kb = mix_a.shape[1], mix_b.shape[1]
    tm = _row_tile(m, mod3, rows_per_group, 512)
    tn = d
    return pl.pallas_call(
        _outproj_kernel,
        grid=(m // tm, d // tn),
        in_specs=[
            pl.BlockSpec((tm, tn), lambda i, j: (i, j)),
            _mod_spec(mod3, 2, d, tm, rows_per_group, tn),
            pl.BlockSpec((tm, ka), lambda i, j: (i, 0)),
            pl.BlockSpec((tm, kb), lambda i, j: (i, 0)),
            pl.BlockSpec((ka, tn), lambda i, j: (0, j)),
            pl.BlockSpec((kb, tn), lambda i, j: (0, j)),
        ],
        out_specs=pl.BlockSpec((tm, tn), lambda i, j: (i, j)),
        out_shape=jax.ShapeDtypeStruct((m, d), F32),
        compiler_params=_cparams(("parallel", "arbitrary")),
        name="out_proj",
    )(x, mod3, mix_a, mix_b, w_a, w_b)


def _page_copies(pt_ref, ck_ref, cv_ref, kbuf, vbuf, sem, step, slot, *, layer, ppg, n_pages):
    spb = n_pages // ppg
    b = lax.div(step, spb)
    s = lax.rem(step, spb)
    copies = []
    for p in range(ppg):
        phys = pt_ref[b * n_pages + (n_pages - 1 - (s * ppg + p))]
        copies.append(pltpu.make_async_copy(ck_ref.at[layer, phys], kbuf.at[slot, p], sem.at[slot]))
        copies.append(pltpu.make_async_copy(cv_ref.at[layer, phys], vbuf.at[slot, p], sem.at[slot]))
    return copies


def _ffn_kernel(*refs, final_norm, paged):
    if paged is None:
        x_ref, nw_ref, sh_ref, sc_ref, ga_ref, fw_ref, wg_ref, wu_ref, wd_ref, o_ref, h_scr, acc_scr = refs
    else:
        (pt_ref, x_ref, nw_ref, sh_ref, sc_ref, ga_ref, fw_ref, wg_ref, wu_ref, wd_ref, q_ref, knew_ref, vnew_ref,
         brow_ref, bcol_ref, ck_ref, cv_ref, o_ref, osb_ref, h_scr, acc_scr, kbuf, vbuf, carry_scr, sem) = refs
        page, ppg, n_pages = paged["page"], paged["ppg"], paged["n_pages"]
        spb = n_pages // ppg
        n_steps = paged["nb"] * spb
        copies = functools.partial(_page_copies, pt_ref, ck_ref, cv_ref, kbuf, vbuf, sem, layer=paged["layer"],
                                   ppg=ppg, n_pages=n_pages)
        t = pl.program_id(0) * pl.num_programs(1) + pl.program_id(1)
        live = t < n_steps
        tc = jnp.minimum(t, n_steps - 1)
        slot = lax.rem(tc, 2)
        b = lax.div(tc, spb)

        @pl.when(t == 0)
        def _():
            for c in copies(0, 0):
                c.start()

        @pl.when(t + 1 < n_steps)
        def _():
            for c in copies(t + 1, lax.rem(t + 1, 2)):
                c.start()

        @pl.when(live)
        def _():
            for c in copies(t, slot):
                c.wait()

        @pl.when(live & (lax.rem(tc, spb) == 0))
        def _():
            osb_ref[b], carry_scr[...] = _paged_self_term(q_ref[b], knew_ref[b], vnew_ref[b], bcol_ref[...],
                                                          page, n_pages)

    f = pl.program_id(1)

    @pl.when(f == 0)
    def _():
        h = _rms(x_ref[...]) * nw_ref[...]
        h_scr[...] = (h * (1.0 + sc_ref[0]) + sh_ref[0]).astype(BF16)
        acc_scr[...] = jnp.zeros_like(acc_scr)

    h = h_scr[...]
    g = jnp.dot(h, wg_ref[...], preferred_element_type=F32)
    u = jnp.dot(h, wu_ref[...], preferred_element_type=F32)
    acc_scr[...] += jnp.dot((_silu(g) * u).astype(BF16), wd_ref[...], preferred_element_type=F32)

    if paged is not None:
        run, acc = _paged_pages(q_ref[b], brow_ref[...], [kbuf[slot, p] for p in range(ppg)],
                                [vbuf[slot, p] for p in range(ppg)], carry_scr[...], osb_ref[b])
        carry_scr[...] = jnp.where(live, run, carry_scr[...])
        osb_ref[b] = jnp.where(live, acc, osb_ref[b])

    @pl.when(f == pl.num_programs(1) - 1)
    def _():
        y = x_ref[...] + ga_ref[0] * acc_scr[...]
        if final_norm:
            y = _rms(y) * fw_ref[...]
        o_ref[...] = y


def _ffn_tiles(m, mod3, rows_per_group, ff, hosts_pages):
    return _row_tile(m, mod3, rows_per_group, 512), _tile(ff, 256 if hosts_pages else 512)


def _ffn(x, nw, mod3, rows_per_group, fw, w_gate, w_up, w_down, final_norm, paged=None):
    m, d = x.shape
    ff = w_gate.shape[1]
    tm, tf = _ffn_tiles(m, mod3, rows_per_group, ff, paged is not None)
    specs = [
        pl.BlockSpec((tm, d), lambda i, f, *_: (i, 0)),
        pl.BlockSpec((1, d), lambda i, f, *_: (0, 0)),
        _mod_spec(mod3, 3, d, tm, rows_per_group),
        _mod_spec(mod3, 4, d, tm, rows_per_group),
        _mod_spec(mod3, 5, d, tm, rows_per_group),
        pl.BlockSpec((1, d), lambda i, f, *_: (0, 0)),
        pl.BlockSpec((d, tf), lambda i, f, *_: (0, f)),
        pl.BlockSpec((d, tf), lambda i, f, *_: (0, f)),
        pl.BlockSpec((tf, d), lambda i, f, *_: (f, 0)),
    ]
    args = [x, nw.reshape(1, d), mod3, mod3, mod3, fw.reshape(1, d), w_gate, w_up, w_down]
    out_spec = pl.BlockSpec((tm, d), lambda i, f, *_: (i, 0))
    out_shape = jax.ShapeDtypeStruct((m, d), F32)
    scratch = [pltpu.VMEM((tm, d), BF16), pltpu.VMEM((tm, d), F32)]
    grid = (m // tm, ff // tf)
    if paged is None:
        return pl.pallas_call(
            functools.partial(_ffn_kernel, final_norm=final_norm, paged=None),
            grid=grid, in_specs=specs, out_specs=out_spec, out_shape=out_shape, scratch_shapes=scratch,
            compiler_params=_cparams(("parallel", "arbitrary")), name="ffn",
        )(*args)

    q, k_new, v_new, bias, cache_k, cache_v, layer, page_table = paged
    nb, nh, hd = q.shape
    n_pool, page = cache_k.shape[1], cache_k.shape[2]
    n_pages = page_table.shape[1]
    ppg = _tile(n_pages, PAGES_PER_STEP)
    width = page * nh
    assert nb * (n_pages // ppg) <= grid[0] * grid[1]
    ck = cache_k.reshape(cache_k.shape[0], n_pool, width, hd)
    cv = cache_v.reshape(cache_v.shape[0], n_pool, width, hd)
    brow = jnp.tile(bias.astype(F32), page).reshape(1, width)
    whole = lambda shape: pl.BlockSpec(shape, lambda i, f, *_: (0,) * len(shape))
    info = dict(layer=layer, page=page, ppg=ppg, n_pages=n_pages, nb=nb)
    grid_spec = pltpu.PrefetchScalarGridSpec(
        num_scalar_prefetch=1,
        grid=grid,
        in_specs=specs + [whole((nb, nh, hd))] * 3 + [whole((1, width)), whole((nh, 1)),
                                                       pl.BlockSpec(memory_space=pl.ANY),
                                                       pl.BlockSpec(memory_space=pl.ANY)],
        out_specs=[out_spec, whole((nb, nh, hd))],
        scratch_shapes=scratch + [pltpu.VMEM((2, ppg, width, hd), F32), pltpu.VMEM((2, ppg, width, hd), F32),
                                  pltpu.VMEM((nh, LANES), F32), pltpu.SemaphoreType.DMA((2,))],
    )
    return pl.pallas_call(
        functools.partial(_ffn_kernel, final_norm=final_norm, paged=info),
        grid_spec=grid_spec,
        out_shape=[out_shape, jax.ShapeDtypeStruct((nb, nh, hd), F32)],
        compiler_params=_cparams(("arbitrary", "arbitrary")),
        name="ffn_paged",
    )(page_table.reshape(-1).astype(jnp.int32), *args, q, k_new, v_new, brow, bias.astype(F32).reshape(nh, 1), ck, cv)


def kernel(x_prompt, x_sample, c_prompt, c_sample, cache_k, cache_v, state_gdn, state_conv, page_table, norm1_w, norm2_w, w_ada, b_ada, w_in, conv_w, a_log, dt_bias, gdn_norm_w, sb_bias, w_out, w_gate, w_up, w_down, final_norm_w):
    depth = w_in.shape[0]
    bp, seq, d = x_prompt.shape
    bs, seq_s, _ = x_sample.shape
    assert seq_s == 1, "the sample group advances one token per step"
    nh_g, nh_s = a_log.shape[1], sb_bias.shape[1]
    hd = gdn_norm_w.shape[1]
    gw, sw = nh_g * hd, nh_s * hd
    kw = conv_w.shape[1]
    c_ba = 4 * gw
    c_sb = c_ba + 2 * nh_g
    n_bat = -(-2 * nh_g // SUBLANES) * SUBLANES

    hp = x_prompt.reshape(bp * seq, d)
    hs = x_sample.reshape(bs, d)
    outs = {n: [] for n in ("kp", "vp", "gp", "cp", "ks", "vs", "gs", "cs")}
    for l in range(depth):
        w_in_l = w_in[l].astype(BF16)
        w_main = jnp.concatenate([w_in_l[:, :c_ba], w_in_l[:, c_sb:]], axis=1)
        w_ba = jnp.zeros((d, LANES), BF16).at[:, :2 * nh_g].set(w_in_l[:, c_ba:c_sb])
        wo_a, wo_b = w_out[l][:gw].astype(BF16), w_out[l][gw:].astype(BF16)
        wg, wu, wd = w_gate[l].astype(BF16), w_up[l].astype(BF16), w_down[l].astype(BF16)
        last = l == depth - 1

        mod_p, mod_s = _ada(c_prompt, c_sample, w_ada[l], b_ada[l])
        mod_p3 = mod_p.reshape(bp, 1, 6 * d)
        mod_s3 = mod_s.reshape(1, bs, 6 * d)

        g_s, q_s, k_s, v_s, ba_s = _inproj(hs, norm1_w[l], mod_s3, 1, w_main, w_ba, 4 * gw, sw, 0)
        paged = (q_s.reshape(bs, nh_s, hd), k_s.reshape(bs, nh_s, hd), v_s.reshape(bs, nh_s, hd), sb_bias[l],
                 cache_k, cache_v, l, page_table)
        tm_p, tf_p = _ffn_tiles(bp * seq, mod_p3, seq, wg.shape[1], True)
        n_pages = page_table.shape[1]
        hosted = bs * (n_pages // _tile(n_pages, PAGES_PER_STEP)) <= (bp * seq // tm_p) * (wg.shape[1] // tf_p)

        g_p, q_p, k_p, v_p, ba_p, bat_p = _inproj(hp, norm1_w[l], mod_p3, seq, w_main, w_ba, 4 * gw, sw, n_bat)
        gdn_o, gdn_s, conv_o = _gdn_chunked(
            g_p, ba_p, bat_p, jnp.zeros((bp, kw - 1, 3 * gw), F32), jnp.zeros((bp, nh_g, hd, hd), F32),
            conv_w[l], a_log[l], dt_bias[l], gdn_norm_w[l], bp, seq)
        sb_o = _sb_prompt(q_p, k_p, v_p, sb_bias[l], bp, seq, hd)
        hp = _outproj(hp, mod_p3, seq, gdn_o, sb_o, wo_a, wo_b)
        if hosted:
            hp, sb_s = _ffn(hp, norm2_w[l], mod_p3, seq, final_norm_w, wg, wu, wd, last, paged)
        else:
            hp = _ffn(hp, norm2_w[l], mod_p3, seq, final_norm_w, wg, wu, wd, last)
            sb_s = _sb_paged(*paged)
        outs["kp"].append(k_p.reshape(bp, seq, nh_s, hd))
        outs["vp"].append(v_p.reshape(bp, seq, nh_s, hd))
        outs["gp"].append(gdn_s)
        outs["cp"].append(conv_o)

        gdn_o, gdn_s = _gdn_step(g_s, ba_s, state_conv[l], state_gdn[l], conv_w[l], a_log[l], dt_bias[l],
                                 gdn_norm_w[l])
        hs = _outproj(hs, mod_s3, 1, gdn_o, sb_s.reshape(bs, sw).astype(BF16), wo_a, wo_b)
        hs = _ffn(hs, norm2_w[l], mod_s3, 1, final_norm_w, wg, wu, wd, last)
        outs["ks"].append(k_s.reshape(bs, 1, nh_s, hd))
        outs["vs"].append(v_s.reshape(bs, 1, nh_s, hd))
        outs["gs"].append(gdn_s.astype(state_gdn.dtype))
        outs["cs"].append(jnp.concatenate([state_conv[l][:, 1:], g_s[:, None, :3 * gw]], axis=1).astype(state_conv.dtype))

    st = lambda n: jnp.stack(outs[n])
    return (hp.reshape(bp, seq, d), hs.reshape(bs, 1, d), st("kp"), st("vp"), st("gp"), st("cp"),
            st("ks"), st("vs"), st("gs"), st("cs"))
```

```python
import functools
import math

import jax
import jax.numpy as jnp
from jax import lax
from jax.experimental import pallas as pl
from jax.experimental.pallas import tpu as pltpu

F32 = jnp.float32
BF16 = jnp.bfloat16
NORM_EPS = 1e-6
LOG2E = 1.4426950408889634
LANES = 128
SUBLANES = 8
GDN_CHUNK = 128
INV_BLOCK = 16
SB_BLOCK = 256
SB_HEADS_PER_STEP = 8
PAGES_PER_STEP = 8
VMEM_LIMIT = 56 * 1024 * 1024


def _cparams(sem):
    return pltpu.CompilerParams(dimension_semantics=sem, vmem_limit_bytes=VMEM_LIMIT)


def _tile(n, pref):
    if n <= pref:
        return n
    t = pref
    while n % t:
        t //= 2
    return t


def _mm(a, b):
    return jnp.dot(a.astype(BF16), b.astype(BF16), preferred_element_type=F32)


def _mm_nt(a, b):
    return lax.dot_general(a.astype(BF16), b.astype(BF16), (((1,), (1,)), ((), ())), preferred_element_type=F32)


def _mm_tn(a, b):
    return lax.dot_general(a.astype(BF16), b.astype(BF16), (((0,), (0,)), ((), ())), preferred_element_type=F32)


def _split3(x):
    hi = x.astype(BF16)
    r = x - hi.astype(F32)
    mid = r.astype(BF16)
    lo = (r - mid.astype(F32)).astype(BF16)
    return hi, mid, lo


def _dot_01_rhs(x, m01):
    return sum(jnp.dot(p, m01, preferred_element_type=F32) for p in _split3(x))


def _dot_01_lhs(m01, x):
    return sum(jnp.dot(m01, p, preferred_element_type=F32) for p in _split3(x))


def _softplus(x):
    return jnp.maximum(x, 0.0) + jnp.log(1.0 + jnp.exp2(jnp.abs(x) * (-LOG2E)))


def _bmm(a, b):
    return lax.dot_general(a.astype(BF16), b.astype(BF16), (((2,), (1,)), ((0,), (0,))), preferred_element_type=F32)


def _bmm_nt(a, b):
    return lax.dot_general(a.astype(BF16), b.astype(BF16), (((2,), (2,)), ((0,), (0,))), preferred_element_type=F32)


def _bmm_tn(a, b):
    return lax.dot_general(a.astype(BF16), b.astype(BF16), (((1,), (1,)), ((0,), (0,))), preferred_element_type=F32)


def _silu(x):
    return x * jax.nn.sigmoid(x)


def _rms(x):
    return x * lax.rsqrt(jnp.mean(x * x, axis=-1, keepdims=True) + NORM_EPS)


def _prep_w_in_kernel(a_ref, b_ref, o_ref, oba_ref, *, n_direct, shift, n_small):
    j = pl.program_id(0)

    @pl.when(j < n_direct)
    def _():
        o_ref[...] = a_ref[...].astype(BF16)

    lane = lax.broadcasted_iota(jnp.int32, a_ref.shape, 1)

    @pl.when(j >= n_direct)
    def _():
        a = pltpu.roll(a_ref[...], LANES - shift, axis=1)
        b = pltpu.roll(b_ref[...], LANES - shift, axis=1)
        o_ref[...] = jnp.where(lane < LANES - shift, a, b).astype(BF16)

    @pl.when(j == n_direct)
    def _():
        oba_ref[...] = jnp.where(lane < n_small, a_ref[...], 0.0).astype(BF16)


def _prep_w_in(w, c_small, n_small):
    d, n = w.shape
    assert c_small % LANES == 0 and 0 < n_small < LANES and (n - n_small) % LANES == 0
    n_direct = c_small // LANES
    n_out = (n - n_small) // LANES
    kern = functools.partial(_prep_w_in_kernel, n_direct=n_direct, shift=n_small, n_small=n_small)
    return pl.pallas_call(
        kern,
        grid=(n_out,),
        in_specs=[
            pl.BlockSpec((d, LANES), lambda j: (0, j)),
            pl.BlockSpec((d, LANES), lambda j: (0, jnp.where(j < n_direct, 0, j + 1))),
        ],
        out_specs=[pl.BlockSpec((d, LANES), lambda j: (0, j)), pl.BlockSpec((d, LANES), lambda j: (0, 0))],
        out_shape=[jax.ShapeDtypeStruct((d, n - n_small), BF16), jax.ShapeDtypeStruct((d, LANES), BF16)],
        compiler_params=_cparams(("arbitrary",)),
        name="prep_w_in",
    )(w, w)


def _ada_kernel(cp_ref, cs_ref, w_ref, b_ref, op_ref, os_ref):
    w = w_ref[...].astype(BF16)
    b = b_ref[...]
    for c_ref, o_ref in ((cp_ref, op_ref), (cs_ref, os_ref)):
        o_ref[...] = jnp.dot(_silu(c_ref[...]).astype(BF16), w, preferred_element_type=F32) + b


def _ada(c_p, c_s, w, b):
    d, n = w.shape
    tn = _tile(n, 1024)
    bp, bs = c_p.shape[0], c_s.shape[0]
    return pl.pallas_call(
        _ada_kernel,
        grid=(n // tn,),
        in_specs=[
            pl.BlockSpec((bp, d), lambda j: (0, 0)),
            pl.BlockSpec((bs, d), lambda j: (0, 0)),
            pl.BlockSpec((d, tn), lambda j: (0, j)),
            pl.BlockSpec((1, tn), lambda j: (0, j)),
        ],
        out_specs=[pl.BlockSpec((bp, tn), lambda j: (0, j)), pl.BlockSpec((bs, tn), lambda j: (0, j))],
        out_shape=[jax.ShapeDtypeStruct((bp, n), F32), jax.ShapeDtypeStruct((bs, n), F32)],
        compiler_params=_cparams(("parallel",)),
        name="ada_mod",
    )(c_p, c_s, w, b.reshape(1, n))


def _row_tile(m, mod3, rows_per_group, pref):
    return m if mod3.shape[1] != 1 else _tile(rows_per_group, pref)


def _mod_spec(mod3, comp, d, tm, rows_per_group, tn=None):
    per_row = mod3.shape[1] != 1
    assert not per_row or mod3.shape[1] == tm
    if tn is None:
        col = lambda j: comp
        tn = d
    else:
        col = lambda j: comp * (d // tn) + j
    if per_row:
        return pl.BlockSpec((1, tm, tn), lambda i, j, *_: (0, 0, col(j)))
    return pl.BlockSpec((1, 1, tn), lambda i, j, *_: (i * tm // rows_per_group, 0, col(j)))


def _inproj_kernel(x_ref, nw_ref, sc_ref, sh_ref, w_ref, wba_ref, og_ref, oq_ref, ok_ref, ov_ref, oba_ref, *rest,
                   n_g, n_h, n_bat):
    obat_ref = rest[0] if n_bat else None
    h_scr = rest[-1]
    j = pl.program_id(1)

    @pl.when(j == 0)
    def _():
        h = _rms(x_ref[...]) * nw_ref[...]
        h = (h * (1.0 + sc_ref[0]) + sh_ref[0]).astype(BF16)
        h_scr[...] = h
        ba = jnp.dot(h, wba_ref[...], preferred_element_type=F32)
        oba_ref[...] = ba
        if n_bat:
            obat_ref[...] = ba.T[:n_bat]

    acc = jnp.dot(h_scr[...], w_ref[...], preferred_element_type=F32)
    for ref, lo, hi in ((og_ref, 0, n_g), (oq_ref, n_g, n_g + n_h), (ok_ref, n_g + n_h, n_g + 2 * n_h),
                        (ov_ref, n_g + 2 * n_h, n_g + 3 * n_h)):
        @pl.when((j >= lo) & (j < hi))
        def _(ref=ref):
            ref[...] = acc


def _inproj(x, nw, mod3, rows_per_group, w_main, w_ba, gdn_cols, sb_cols, n_bat):
    m, d = x.shape
    tm = _row_tile(m, mod3, rows_per_group, 1024)
    tn = _tile(math.gcd(gdn_cols, sb_cols), 512)
    n_g, n_h = gdn_cols // tn, sb_cols // tn
    nj = n_g + 3 * n_h

    def seg(lo, n):
        return lambda i, j: (i, jnp.clip(j - lo, 0, n - 1))

    kern = functools.partial(_inproj_kernel, n_g=n_g, n_h=n_h, n_bat=n_bat)
    return pl.pallas_call(
        kern,
        grid=(m // tm, nj),
        in_specs=[
            pl.BlockSpec((tm, d), lambda i, j: (i, 0)),
            pl.BlockSpec((1, d), lambda i, j: (0, 0)),
            _mod_spec(mod3, 1, d, tm, rows_per_group),
            _mod_spec(mod3, 0, d, tm, rows_per_group),
            pl.BlockSpec((d, tn), lambda i, j: (0, j)),
            pl.BlockSpec((d, LANES), lambda i, j: (0, 0)),
        ],
        out_specs=[
            pl.BlockSpec((tm, tn), seg(0, n_g)),
            pl.BlockSpec((tm, tn), seg(n_g, n_h)),
            pl.BlockSpec((tm, tn), seg(n_g + n_h, n_h)),
            pl.BlockSpec((tm, tn), seg(n_g + 2 * n_h, n_h)),
            pl.BlockSpec((tm, LANES), lambda i, j: (i, 0)),
        ] + ([pl.BlockSpec((n_bat, tm), lambda i, j: (0, i))] if n_bat else []),
        out_shape=[
            jax.ShapeDtypeStruct((m, gdn_cols), F32),
            jax.ShapeDtypeStruct((m, sb_cols), F32),
            jax.ShapeDtypeStruct((m, sb_cols), F32),
            jax.ShapeDtypeStruct((m, sb_cols), F32),
            jax.ShapeDtypeStruct((m, LANES), F32),
        ] + ([jax.ShapeDtypeStruct((n_bat, m), F32)] if n_bat else []),
        scratch_shapes=[pltpu.VMEM((tm, d), BF16)],
        compiler_params=_cparams(("parallel", "arbitrary")),
        name="in_proj",
    )(x, nw.reshape(1, d), mod3, mod3, w_main, w_ba)


def _unit_lower_inverse_minus_eye(a, ri, ci):
    c = a.shape[-1]
    blk = ((ri // INV_BLOCK) == (ci // INV_BLOCK))[None]
    ad = jnp.where(blk, a, 0.0)
    off = a - ad
    xn = -ad
    p = ad
    for _ in range(int(math.log2(INV_BLOCK)) - 1):
        p = _bmm(p, p)
        xn = xn + p + _bmm(xn, p)
    m = off + _bmm(xn, off)
    yn = -m
    p = m
    for _ in range(int(math.log2(c // INV_BLOCK)) - 1):
        p = _bmm(p, p)
        yn = yn + p + _bmm(yn, p)
    return yn + xn + _bmm(yn, xn)


def _gdn_chunk_kernel(g_ref, ba_ref, bat_ref, cb_ref, s0_ref, cw_ref, arow_ref, drow_ref, acol_ref, dcol_ref, nw_ref,
                      o_ref, s_ref, hist_ref, *, nh, hd, chunk, kw):
    c = pl.program_id(1)

    @pl.when(c == 0)
    def _():
        s_ref[...] = s0_ref[...]
        hist_ref[...] = cb_ref[...]

    gw = nh * hd
    ri = lax.broadcasted_iota(jnp.int32, (chunk, chunk), 0)
    ci = lax.broadcasted_iota(jnp.int32, (chunk, chunk), 1)
    incl = ci <= ri
    strict = ci < ri
    ltri = incl.astype(BF16)
    utri = (ri <= ci).astype(BF16)

    ba = ba_ref[...]
    bat = bat_ref[...]
    beta_all = jax.nn.sigmoid(ba)
    g_all = -jnp.exp(arow_ref[...]) * _softplus(ba + drow_ref[...])
    g_t = -jnp.exp(acol_ref[...]) * _softplus(bat + dcol_ref[...])
    gc_all = _dot_01_lhs(ltri, g_all)
    gc_t = _dot_01_rhs(g_t, utri)
    scale = hd ** -0.5

    hist = hist_ref[0]
    x_all = g_ref[...]
    cw = cw_ref[...]
    nw = nw_ref[...]
    pad = hist.shape[0]

    xw = jnp.concatenate([hist, x_all[:, :3 * gw]], axis=0)
    y = cw[kw - 1:kw] * xw[pad:]
    for i in range(kw - 1):
        sft = kw - 1 - i
        y = y + cw[i:i + 1] * xw[pad - sft:pad - sft + chunk]
    y = _silu(y)

    def heads(arr, base):
        return jnp.stack([arr[:, base + h * hd:base + (h + 1) * hd] for h in range(nh)])

    def l2n(t):
        return t * lax.rsqrt(jnp.sum(t * t, axis=-1, keepdims=True) + NORM_EPS)

    q = l2n(heads(y, 0)) * scale
    k = l2n(heads(y, gw))
    v = heads(y, 2 * gw)
    beta = jnp.stack([beta_all[:, h:h + 1] for h in range(nh)])
    gc = jnp.stack([gc_all[:, nh + h:nh + h + 1] for h in range(nh)])
    gc_row = jnp.stack([gc_t[nh + h:nh + h + 1, :] for h in range(nh)])
    gl = gc[:, chunk - 1:chunk, :]
    decay = jnp.exp(jnp.where(incl[None], gc - gc_row, -1e30))
    kb = k * beta
    a = jnp.where(strict[None], _bmm_nt(kb, k) * decay, 0.0)
    tn = _unit_lower_inverse_minus_eye(a, ri, ci)
    egc = jnp.exp(gc)
    r = jnp.concatenate([v * beta, kb * egc], axis=2)
    uw = r + _bmm(tn, r)
    u, w = uw[:, :, :hd], uw[:, :, hd:]
    qk = jnp.where(incl[None], _bmm_nt(q, k) * decay, 0.0)
    s = s_ref[0]
    v_new = u - _bmm(w, s)
    o = _bmm(q * egc, s) + _bmm(qk, v_new)
    s_ref[0] = s * jnp.exp(gl) + _bmm_tn(k * jnp.exp(gl - gc), v_new)
    out = (_rms(o) * nw * _silu(heads(x_all, 3 * gw))).astype(o_ref.dtype)
    for h in range(nh):
        o_ref[:, h * hd:(h + 1) * hd] = out[h]

    hist_ref[0] = x_all[chunk - pad:, :3 * gw]


def _gdn_chunked(gdn, ba, bat, conv_buf, state0, conv_w, a_log, dt_bias, norm_w, batch, seq):
    m, gcols = gdn.shape
    nh = a_log.shape[0]
    hd = norm_w.shape[0]
    gw = nh * hd
    kw = conv_w.shape[0]
    chunk = _tile(seq, GDN_CHUNK)
    nc = seq // chunk
    n_bat = bat.shape[0]
    pad = SUBLANES
    cb = jnp.zeros((batch, pad, 3 * gw), F32).at[:, pad - (kw - 1):].set(conv_buf.astype(F32))
    cw = jnp.zeros((pad, 3 * gw), F32).at[:kw].set(conv_w.astype(F32))
    arow = jnp.zeros((1, LANES), F32).at[0, nh:2 * nh].set(a_log)
    drow = jnp.zeros((1, LANES), F32).at[0, nh:2 * nh].set(dt_bias)
    acol = jnp.zeros((n_bat, 1), F32).at[nh:2 * nh, 0].set(a_log)
    dcol = jnp.zeros((n_bat, 1), F32).at[nh:2 * nh, 0].set(dt_bias)
    kern = functools.partial(_gdn_chunk_kernel, nh=nh, hd=hd, chunk=chunk, kw=kw)
    const2 = lambda b, c: (0, 0)
    o, s, hist = pl.pallas_call(
        kern,
        grid=(batch, nc),
        in_specs=[
            pl.BlockSpec((chunk, gcols), lambda b, c: (b * nc + c, 0)),
            pl.BlockSpec((chunk, LANES), lambda b, c: (b * nc + c, 0)),
            pl.BlockSpec((n_bat, chunk), lambda b, c: (0, b * nc + c)),
            pl.BlockSpec((1, pad, 3 * gw), lambda b, c: (b, 0, 0)),
            pl.BlockSpec((1, nh, hd, hd), lambda b, c: (b, 0, 0, 0)),
            pl.BlockSpec((pad, 3 * gw), const2),
            pl.BlockSpec((1, LANES), const2),
            pl.BlockSpec((1, LANES), const2),
            pl.BlockSpec((n_bat, 1), const2),
            pl.BlockSpec((n_bat, 1), const2),
            pl.BlockSpec((1, hd), const2),
        ],
        out_specs=[
            pl.BlockSpec((chunk, gw), lambda b, c: (b * nc + c, 0)),
            pl.BlockSpec((1, nh, hd, hd), lambda b, c: (b, 0, 0, 0)),
            pl.BlockSpec((1, pad, 3 * gw), lambda b, c: (b, 0, 0)),
        ],
        out_shape=[
            jax.ShapeDtypeStruct((m, gw), BF16),
            jax.ShapeDtypeStruct((batch, nh, hd, hd), F32),
            jax.ShapeDtypeStruct((batch, pad, 3 * gw), F32),
        ],
        compiler_params=_cparams(("parallel", "arbitrary")),
        name="gdn_chunked",
    )(gdn, ba, bat, cb, state0.astype(F32), cw, arow, drow, acol, dcol, norm_w.reshape(1, hd).astype(F32))
    return o, s, hist[:, pad - (kw - 1):]


def _gdn_step_kernel(g_ref, ba_ref, cb_ref, s0_ref, cw_ref, arow_ref, drow_ref, nw_ref, o_ref, s_ref, *, nh, hd, kw):
    gw = nh * hd
    x = g_ref[0]
    cb = cb_ref[0]
    cw = cw_ref[...]
    y = cw[kw - 1:kw] * x[:, :3 * gw]
    for i in range(kw - 1):
        y = y + cw[i:i + 1] * cb[i:i + 1]
    y = _silu(y)
    ba = ba_ref[0]
    beta_all = jax.nn.sigmoid(ba)
    g_all = -jnp.exp(arow_ref[...]) * _softplus(ba + drow_ref[...])
    nw = nw_ref[...]
    row = lax.broadcasted_iota(jnp.int32, (SUBLANES, hd), 0)
    scale = hd ** -0.5

    def l2n(t):
        return t * lax.rsqrt(jnp.sum(t * t, axis=-1, keepdims=True) + NORM_EPS)

    for h in range(nh):
        q = l2n(y[:, h * hd:(h + 1) * hd]) * scale
        k = l2n(y[:, gw + h * hd:gw + (h + 1) * hd])
        v = y[:, 2 * gw + h * hd:2 * gw + (h + 1) * hd]
        beta = beta_all[:, h:h + 1]
        eg = jnp.exp(g_all[:, nh + h:nh + h + 1])
        s = s0_ref[0, h]
        lhs = jnp.where(row == 0, jnp.broadcast_to(k, (SUBLANES, hd)),
                        jnp.where(row == 1, jnp.broadcast_to(q, (SUBLANES, hd)), 0.0))
        ks_qs = _mm(lhs, s)
        ks, qs = ks_qs[0:1], ks_qs[1:2]
        v_new = beta * v - (beta * eg) * ks
        qk = jnp.sum(q * k, axis=-1, keepdims=True)
        o = eg * qs + qk * v_new
        k8 = jnp.where(row == 0, jnp.broadcast_to(k, (SUBLANES, hd)), 0.0)
        v8 = jnp.where(row == 0, jnp.broadcast_to(v_new, (SUBLANES, hd)), 0.0)
        s_ref[0, h] = s * eg + _mm_tn(k8, v8)
        z = x[:, 3 * gw + h * hd:3 * gw + (h + 1) * hd]
        o_ref[0, :, h * hd:(h + 1) * hd] = (_rms(o) * nw * _silu(z)).astype(o_ref.dtype)


def _gdn_step(gdn, ba, conv_buf, state0, conv_w, a_log, dt_bias, norm_w):
    b, gcols = gdn.shape
    nh = a_log.shape[0]
    hd = norm_w.shape[0]
    gw = nh * hd
    kw = conv_w.shape[0]
    arow = jnp.zeros((1, LANES), F32).at[0, nh:2 * nh].set(a_log)
    drow = jnp.zeros((1, LANES), F32).at[0, nh:2 * nh].set(dt_bias)
    kern = functools.partial(_gdn_step_kernel, nh=nh, hd=hd, kw=kw)
    const2 = lambda i: (0, 0)
    o, s = pl.pallas_call(
        kern,
        grid=(b,),
        in_specs=[
            pl.BlockSpec((1, 1, gcols), lambda i: (i, 0, 0)),
            pl.BlockSpec((1, 1, LANES), lambda i: (i, 0, 0)),
            pl.BlockSpec((1, kw - 1, 3 * gw), lambda i: (i, 0, 0)),
            pl.BlockSpec((1, nh, hd, hd), lambda i: (i, 0, 0, 0)),
            pl.BlockSpec((kw, 3 * gw), const2),
            pl.BlockSpec((1, LANES), const2),
            pl.BlockSpec((1, LANES), const2),
            pl.BlockSpec((1, hd), const2),
        ],
        out_specs=[
            pl.BlockSpec((1, 1, gw), lambda i: (i, 0, 0)),
            pl.BlockSpec((1, nh, hd, hd), lambda i: (i, 0, 0, 0)),
        ],
        out_shape=[jax.ShapeDtypeStruct((b, 1, gw), BF16), jax.ShapeDtypeStruct((b, nh, hd, hd), F32)],
        compiler_params=_cparams(("parallel",)),
        name="gdn_step",
    )(gdn.reshape(b, 1, gcols), ba.reshape(b, 1, LANES), conv_buf.astype(F32), state0.astype(F32),
      conv_w.astype(F32), arow, drow, norm_w.reshape(1, hd).astype(F32))
    return o.reshape(b, gw), s


def _sb_block_update(q, k, v, bias, carry, acc, mask, utri):
    g, tq, _ = q.shape
    tk = k.shape[1]
    zb = _bmm_nt(q, k)
    z = jnp.stack([zb[i] + bias[i] for i in range(g)])
    nlk = _softplus(z)
    if mask is not None:
        nlk = jnp.where(mask[None], nlk, 0.0)
    hi = nlk.astype(BF16)
    lo = (nlk - hi.astype(F32)).astype(BF16)
    hilo = jnp.concatenate([hi, lo], axis=2).reshape(g * tq, 2 * tk)
    from_here = jnp.dot(hilo, utri, preferred_element_type=F32).reshape(g, tq, tk)
    w = jnp.exp(z - from_here - carry)
    if mask is not None:
        w = jnp.where(mask[None], w, 0.0)
    acc = acc + _bmm(w, v)
    return carry + jnp.sum(nlk, axis=-1, keepdims=True), acc


def _sb_prompt_kernel(bias_ref, q_ref, k_ref, v_ref, o_ref, *, blk, hd, hp):
    qi = pl.program_id(2)
    hg = pl.program_id(1)
    bias = [bias_ref[hg * hp + i] for i in range(hp)]
    scale = hd ** -0.5
    q = jnp.stack([(q_ref[:, i * hd:(i + 1) * hd] * scale).astype(BF16) for i in range(hp)])
    ri = lax.broadcasted_iota(jnp.int32, (blk, blk), 0)
    ci = lax.broadcasted_iota(jnp.int32, (blk, blk), 1)
    utri = (ri >= ci).astype(BF16)
    utri = jnp.concatenate([utri, utri], axis=0)

    def heads(ref, start):
        return jnp.stack([ref[pl.ds(start, blk), i * hd:(i + 1) * hd] for i in range(hp)])

    start = pl.multiple_of(qi * blk, blk)
    carry, acc = _sb_block_update(q, heads(k_ref, start), heads(v_ref, start), bias,
                                  jnp.zeros((hp, blk, 1), F32), jnp.zeros((hp, blk, hd), F32), ci < ri, utri)

    def body(it, ca):
        s0 = pl.multiple_of((qi - 1 - it) * blk, blk)
        return _sb_block_update(q, heads(k_ref, s0), heads(v_ref, s0), bias, ca[0], ca[1], None, utri)

    carry, acc = lax.fori_loop(0, qi, body, (carry, acc))
    for i in range(hp):
        o_ref[:, i * hd:(i + 1) * hd] = acc[i].astype(o_ref.dtype)


def _sb_prompt(q, k, v, bias, batch, seq, hd):
    m, width = q.shape
    nh = width // hd
    hp = _tile(nh, SB_HEADS_PER_STEP)
    blk = _tile(seq, SB_BLOCK)
    nq = seq // blk
    kern = functools.partial(_sb_prompt_kernel, blk=blk, hd=hd, hp=hp)
    return pl.pallas_call(
        kern,
        grid=(batch, nh // hp, nq),
        in_specs=[
            pl.BlockSpec(memory_space=pltpu.SMEM),
            pl.BlockSpec((blk, hp * hd), lambda b, h, i: (b * nq + i, h)),
            pl.BlockSpec((seq, hp * hd), lambda b, h, i: (b, h)),
            pl.BlockSpec((seq, hp * hd), lambda b, h, i: (b, h)),
        ],
        out_specs=pl.BlockSpec((blk, hp * hd), lambda b, h, i: (b * nq + i, h)),
        out_shape=jax.ShapeDtypeStruct((m, width), BF16),
        compiler_params=_cparams(("parallel", "parallel", "arbitrary")),
        name="sb_prompt",
    )(bias.astype(F32), q, k, v)


def _paged_self_term(q, k_new, v_new, bcol, page, n_pages):
    nh, hd = q.shape
    z = jnp.sum(q * k_new, axis=-1, keepdims=True) * (hd ** -0.5) + bcol
    pos = jnp.full((nh, 1), n_pages * page, jnp.int32)
    visible = pos < pos
    nlk = jnp.where(visible, _softplus(z), 0.0)
    w = jnp.where(visible, jnp.exp(z - nlk), 0.0)
    return w * v_new, jnp.broadcast_to(nlk, (nh, LANES))


def _paged_pages(q, brow, k_pages, v_pages, run, acc):
    nh, hd = q.shape
    width = brow.shape[1]
    nt = width // LANES
    sub = lax.broadcasted_iota(jnp.int32, (nh, width), 0)
    lane = lax.broadcasted_iota(jnp.int32, (nh, width), 1)
    own = (lane % nh) == sub
    ri = lax.broadcasted_iota(jnp.int32, (LANES, 2 * LANES), 0)
    ci = lax.broadcasted_iota(jnp.int32, (LANES, 2 * LANES), 1)
    usum = ((ri > ci) | (ci >= LANES)).astype(BF16)
    qs = (q * (hd ** -0.5)).astype(BF16)
    z = [_mm_nt(qs, kp) + brow for kp in k_pages]
    nlk = [jnp.where(own, _softplus(zp), 0.0) for zp in z]
    tiles = jnp.concatenate([n[:, t * LANES:(t + 1) * LANES] for n in nlk for t in range(nt)], axis=0)
    hi = tiles.astype(BF16)
    lo = (tiles - hi.astype(F32)).astype(BF16)
    su = jnp.dot(hi, usum, preferred_element_type=F32) + jnp.dot(lo, usum, preferred_element_type=F32)
    for p, vp in enumerate(v_pages):
        later = []
        for t in reversed(range(nt)):
            r0 = (p * nt + t) * nh
            later.append(su[r0:r0 + nh, :LANES] + run)
            run = run + su[r0:r0 + nh, LANES:]
        later = jnp.concatenate(later[::-1], axis=1)
        w = jnp.where(own, jnp.exp(z[p] - nlk[p] - later), 0.0)
        acc = acc + _mm(w, vp)
    return run, acc


def _sb_paged_kernel(pt_ref, q_ref, knew_ref, vnew_ref, brow_ref, bcol_ref, *refs, page, npp, n_pages):
    k_refs, v_refs = refs[:npp], refs[npp:2 * npp]
    o_ref, carry_ref = refs[2 * npp], refs[2 * npp + 1]

    @pl.when(pl.program_id(1) == 0)
    def _():
        o_ref[0], carry_ref[...] = _paged_self_term(q_ref[0], knew_ref[0], vnew_ref[0], bcol_ref[...], page, n_pages)

    carry_ref[...], o_ref[0] = _paged_pages(q_ref[0], brow_ref[...], [r[...] for r in k_refs],
                                             [r[...] for r in v_refs], carry_ref[...], o_ref[0])


def _sb_paged(q, k_new, v_new, bias, cache_k, cache_v, layer, page_table):
    b, nh, hd = q.shape
    n_pool, page = cache_k.shape[1], cache_k.shape[2]
    n_pages = page_table.shape[1]
    npp = _tile(n_pages, PAGES_PER_STEP)
    width = page * nh
    ck = cache_k.reshape(cache_k.shape[0], n_pool, width, hd)
    cv = cache_v.reshape(cache_v.shape[0], n_pool, width, hd)
    brow = jnp.tile(bias.astype(F32), page).reshape(1, width)

    def page_spec(p):
        return pl.BlockSpec((None, None, width, hd),
                            lambda i, s, pt: (layer, pt[i * n_pages + n_pages - 1 - (s * npp + p)], 0, 0))

    kern = functools.partial(_sb_paged_kernel, page=page, npp=npp, n_pages=n_pages)
    head_spec = pl.BlockSpec((1, nh, hd), lambda i, s, pt: (i, 0, 0))
    grid_spec = pltpu.PrefetchScalarGridSpec(
        num_scalar_prefetch=1,
        grid=(b, n_pages // npp),
        in_specs=[head_spec, head_spec, head_spec, pl.BlockSpec((1, width), lambda i, s, pt: (0, 0)),
                  pl.BlockSpec((nh, 1), lambda i, s, pt: (0, 0))]
        + [page_spec(p) for p in range(npp)] + [page_spec(p) for p in range(npp)],
        out_specs=head_spec,
        scratch_shapes=[pltpu.VMEM((nh, LANES), F32)],
    )
    return pl.pallas_call(
        kern,
        grid_spec=grid_spec,
        out_shape=jax.ShapeDtypeStruct((b, nh, hd), F32),
        compiler_params=_cparams(("parallel", "arbitrary")),
        name="sb_paged",
    )(page_table.reshape(-1).astype(jnp.int32), q, k_new, v_new, brow, bias.astype(F32).reshape(nh, 1),
      *([ck] * npp), *([cv] * npp))


def _outproj_kernel(x_ref, ga_ref, a_ref, b_ref, wa_ref, wb_ref, o_ref):
    y = jnp.dot(a_ref[...], wa_ref[...], preferred_element_type=F32)
    y = y + jnp.dot(b_ref[...], wb_ref[...], preferred_element_type=F32)
    o_ref[...] = x_ref[...] + ga_ref[0] * y


def _outproj(x, mod3, rows_per_group, mix_a, mix_b, w_a, w_b):
    m, d = x.shape
    ka, kb = mix_a.shape[1], mix_b.shape[1]
    tm = _row_tile(m, mod3, rows_per_group, 512)
    tn = d
    return pl.pallas_call(
        _outproj_kernel,
        grid=(m // tm, d // tn),
        in_specs=[
            pl.BlockSpec((tm, tn), lambda i, j: (i, j)),
            _mod_spec(mod3, 2, d, tm, rows_per_group, tn),
            pl.BlockSpec((tm, ka), lambda i, j: (i, 0)),
            pl.BlockSpec((tm, kb), lambda i, j: (i, 0)),
            pl.BlockSpec((ka, tn), lambda i, j: (0, j)),
            pl.BlockSpec((kb, tn), lambda i, j: (0, j)),
        ],
        out_specs=pl.BlockSpec((tm, tn), lambda i, j: (i, j)),
        out_shape=jax.ShapeDtypeStruct((m, d), F32),
        compiler_params=_cparams(("parallel", "arbitrary")),
        name="out_proj",
    )(x, mod3, mix_a, mix_b, w_a, w_b)


def _page_copies(pt_ref, ck_ref, cv_ref, kbuf, vbuf, sem, step, slot, *, layer, ppg, n_pages):
    spb = n_pages // ppg
    b = lax.div(step, spb)
    s = lax.rem(step, spb)
    copies = []
    for p in range(ppg):
        phys = pt_ref[b * n_pages + (n_pages - 1 - (s * ppg + p))]
        copies.append(pltpu.make_async_copy(ck_ref.at[layer, phys], kbuf.at[slot, p], sem.at[slot]))
        copies.append(pltpu.make_async_copy(cv_ref.at[layer, phys], vbuf.at[slot, p], sem.at[slot]))
    return copies


def _ffn_kernel(*refs, final_norm, paged):
    if paged is None:
        x_ref, nw_ref, sh_ref, sc_ref, ga_ref, fw_ref, wg_ref, wu_ref, wd_ref, o_ref, h_scr, acc_scr = refs
    else:
        (pt_ref, x_ref, nw_ref, sh_ref, sc_ref, ga_ref, fw_ref, wg_ref, wu_ref, wd_ref, q_ref, knew_ref, vnew_ref,
         brow_ref, bcol_ref, ck_ref, cv_ref, o_ref, osb_ref, h_scr, acc_scr, kbuf, vbuf, carry_scr, sem) = refs
        page, ppg, n_pages = paged["page"], paged["ppg"], paged["n_pages"]
        spb = n_pages // ppg
        n_steps = paged["nb"] * spb
        copies = functools.partial(_page_copies, pt_ref, ck_ref, cv_ref, kbuf, vbuf, sem, layer=paged["layer"],
                                   ppg=ppg, n_pages=n_pages)
        t = pl.program_id(0) * pl.num_programs(1) + pl.program_id(1)

        @pl.when(t == 0)
        def _():
            for c in copies(0, 0):
                c.start()

        @pl.when(t + 1 < n_steps)
        def _():
            for c in copies(t + 1, lax.rem(t + 1, 2)):
                c.start()

    f = pl.program_id(1)

    @pl.when(f == 0)
    def _():
        h = _rms(x_ref[...]) * nw_ref[...]
        h_scr[...] = (h * (1.0 + sc_ref[0]) + sh_ref[0]).astype(BF16)
        acc_scr[...] = jnp.zeros_like(acc_scr)

    h = h_scr[...]
    g = jnp.dot(h, wg_ref[...], preferred_element_type=F32)
    u = jnp.dot(h, wu_ref[...], preferred_element_type=F32)
    acc_scr[...] += jnp.dot((_silu(g) * u).astype(BF16), wd_ref[...], preferred_element_type=F32)

    @pl.when(f == pl.num_programs(1) - 1)
    def _():
        y = x_ref[...] + ga_ref[0] * acc_scr[...]
        if final_norm:
            y = _rms(y) * fw_ref[...]
        o_ref[...] = y

    if paged is not None:
        @pl.when(t < n_steps)
        def _():
            slot = lax.rem(t, 2)
            for c in copies(t, slot):
                c.wait()
            b = lax.div(t, spb)

            @pl.when(lax.rem(t, spb) == 0)
            def _():
                osb_ref[b], carry_scr[...] = _paged_self_term(q_ref[b], knew_ref[b], vnew_ref[b], bcol_ref[...],
                                                              page, n_pages)

            carry_scr[...], osb_ref[b] = _paged_pages(
                q_ref[b], brow_ref[...], [kbuf[slot, p] for p in range(ppg)], [vbuf[slot, p] for p in range(ppg)],
                carry_scr[...], osb_ref[b])


def _ffn_tiles(m, mod3, rows_per_group, ff, hosts_pages):
    return _row_tile(m, mod3, rows_per_group, 512), _tile(ff, 256 if hosts_pages else 512)


def _ffn(x, nw, mod3, rows_per_group, fw, w_gate, w_up, w_down, final_norm, paged=None):
    m, d = x.shape
    ff = w_gate.shape[1]
    tm, tf = _ffn_tiles(m, mod3, rows_per_group, ff, paged is not None)
    specs = [
        pl.BlockSpec((tm, d), lambda i, f, *_: (i, 0)),
        pl.BlockSpec((1, d), lambda i, f, *_: (0, 0)),
        _mod_spec(mod3, 3, d, tm, rows_per_group),
        _mod_spec(mod3, 4, d, tm, rows_per_group),
        _mod_spec(mod3, 5, d, tm, rows_per_group),
        pl.BlockSpec((1, d), lambda i, f, *_: (0, 0)),
        pl.BlockSpec((d, tf), lambda i, f, *_: (0, f)),
        pl.BlockSpec((d, tf), lambda i, f, *_: (0, f)),
        pl.BlockSpec((tf, d), lambda i, f, *_: (f, 0)),
    ]
    args = [x, nw.reshape(1, d), mod3, mod3, mod3, fw.reshape(1, d), w_gate, w_up, w_down]
    out_spec = pl.BlockSpec((tm, d), lambda i, f, *_: (i, 0))
    out_shape = jax.ShapeDtypeStruct((m, d), F32)
    scratch = [pltpu.VMEM((tm, d), BF16), pltpu.VMEM((tm, d), F32)]
    grid = (m // tm, ff // tf)
    if paged is None:
        return pl.pallas_call(
            functools.partial(_ffn_kernel, final_norm=final_norm, paged=None),
            grid=grid, in_specs=specs, out_specs=out_spec, out_shape=out_shape, scratch_shapes=scratch,
            compiler_params=_cparams(("parallel", "arbitrary")), name="ffn",
        )(*args)

    q, k_new, v_new, bias, cache_k, cache_v, layer, page_table = paged
    nb, nh, hd = q.shape
    n_pool, page = cache_k.shape[1], cache_k.shape[2]
    n_pages = page_table.shape[1]
    ppg = _tile(n_pages, PAGES_PER_STEP)
    width = page * nh
    assert nb * (n_pages // ppg) <= grid[0] * grid[1]
    ck = cache_k.reshape(cache_k.shape[0], n_pool, width, hd)
    cv = cache_v.reshape(cache_v.shape[0], n_pool, width, hd)
    brow = jnp.tile(bias.astype(F32), page).reshape(1, width)
    whole = lambda shape: pl.BlockSpec(shape, lambda i, f, *_: (0,) * len(shape))
    info = dict(layer=layer, page=page, ppg=ppg, n_pages=n_pages, nb=nb)
    grid_spec = pltpu.PrefetchScalarGridSpec(
        num_scalar_prefetch=1,
        grid=grid,
        in_specs=specs + [whole((nb, nh, hd))] * 3 + [whole((1, width)), whole((nh, 1)),
                                                       pl.BlockSpec(memory_space=pl.ANY),
                                                       pl.BlockSpec(memory_space=pl.ANY)],
        out_specs=[out_spec, whole((nb, nh, hd))],
        scratch_shapes=scratch + [pltpu.VMEM((2, ppg, width, hd), F32), pltpu.VMEM((2, ppg, width, hd), F32),
                                  pltpu.VMEM((nh, LANES), F32), pltpu.SemaphoreType.DMA((2,))],
    )
    return pl.pallas_call(
        functools.partial(_ffn_kernel, final_norm=final_norm, paged=info),
        grid_spec=grid_spec,
        out_shape=[out_shape, jax.ShapeDtypeStruct((nb, nh, hd), F32)],
        compiler_params=_cparams(("arbitrary", "arbitrary")),
        name="ffn_paged",
    )(page_table.reshape(-1).astype(jnp.int32), *args, q, k_new, v_new, brow, bias.astype(F32).reshape(nh, 1), ck, cv)


def kernel(x_prompt, x_sample, c_prompt, c_sample, cache_k, cache_v, state_gdn, state_conv, page_table, norm1_w, norm2_w, w_ada, b_ada, w_in, conv_w, a_log, dt_bias, gdn_norm_w, sb_bias, w_out, w_gate, w_up, w_down, final_norm_w):
    depth = w_in.shape[0]
    bp, seq, d = x_prompt.shape
    bs, seq_s, _ = x_sample.shape
    assert seq_s == 1, "the sample group advances one token per step"
    nh_g, nh_s = a_log.shape[1], sb_bias.shape[1]
    hd = gdn_norm_w.shape[1]
    gw, sw = nh_g * hd, nh_s * hd
    kw = conv_w.shape[1]
    c_ba = 4 * gw
    c_sb = c_ba + 2 * nh_g
    n_bat = -(-2 * nh_g // SUBLANES) * SUBLANES

    hp = x_prompt.reshape(bp * seq, d)
    hs = x_sample.reshape(bs, d)
    outs = {n: [] for n in ("kp", "vp", "gp", "cp", "ks", "vs", "gs", "cs")}
    for l in range(depth):
        w_main, w_ba = _prep_w_in(w_in[l], c_ba, c_sb - c_ba)
        wo_a, wo_b = w_out[l][:gw].astype(BF16), w_out[l][gw:].astype(BF16)
        wg, wu, wd = w_gate[l].astype(BF16), w_up[l].astype(BF16), w_down[l].astype(BF16)
        last = l == depth - 1

        mod_p, mod_s = _ada(c_prompt, c_sample, w_ada[l], b_ada[l])
        mod_p3 = mod_p.reshape(bp, 1, 6 * d)
        mod_s3 = mod_s.reshape(1, bs, 6 * d)

        g_s, q_s, k_s, v_s, ba_s = _inproj(hs, norm1_w[l], mod_s3, 1, w_main, w_ba, 4 * gw, sw, 0)
        paged = (q_s.reshape(bs, nh_s, hd), k_s.reshape(bs, nh_s, hd), v_s.reshape(bs, nh_s, hd), sb_bias[l],
                 cache_k, cache_v, l, page_table)
        tm_p, tf_p = _ffn_tiles(bp * seq, mod_p3, seq, wg.shape[1], True)
        n_pages = page_table.shape[1]
        hosted = bs * (n_pages // _tile(n_pages, PAGES_PER_STEP)) <= (bp * seq // tm_p) * (wg.shape[1] // tf_p)

        g_p, q_p, k_p, v_p, ba_p, bat_p = _inproj(hp, norm1_w[l], mod_p3, seq, w_main, w_ba, 4 * gw, sw, n_bat)
        gdn_o, gdn_s, conv_o = _gdn_chunked(
            g_p, ba_p, bat_p, jnp.zeros((bp, kw - 1, 3 * gw), F32), jnp.zeros((bp, nh_g, hd, hd), F32),
            conv_w[l], a_log[l], dt_bias[l], gdn_norm_w[l], bp, seq)
        sb_o = _sb_prompt(q_p, k_p, v_p, sb_bias[l], bp, seq, hd)
        hp = _outproj(hp, mod_p3, seq, gdn_o, sb_o, wo_a, wo_b)
        if hosted:
            hp, sb_s = _ffn(hp, norm2_w[l], mod_p3, seq, final_norm_w, wg, wu, wd, last, paged)
        else:
            hp = _ffn(hp, norm2_w[l], mod_p3, seq, final_norm_w, wg, wu, wd, last)
            sb_s = _sb_paged(*paged)
        outs["kp"].append(k_p.reshape(bp, seq, nh_s, hd))
        outs["vp"].append(v_p.reshape(bp, seq, nh_s, hd))
        outs["gp"].append(gdn_s)
        outs["cp"].append(conv_o)

        gdn_o, gdn_s = _gdn_step(g_s, ba_s, state_conv[l], state_gdn[l], conv_w[l], a_log[l], dt_bias[l],
                                 gdn_norm_w[l])
        hs = _outproj(hs, mod_s3, 1, gdn_o, sb_s.reshape(bs, sw).astype(BF16), wo_a, wo_b)
        hs = _ffn(hs, norm2_w[l], mod_s3, 1, final_norm_w, wg, wu, wd, last)
        outs["ks"].append(k_s.reshape(bs, 1, nh_s, hd))
        outs["vs"].append(v_s.reshape(bs, 1, nh_s, hd))
        outs["gs"].append(gdn_s.astype(state_gdn.dtype))
        outs["cs"].append(jnp.concatenate([state_conv[l][:, 1:], g_s[:, None, :3 * gw]], axis=1).astype(state_conv.dtype))

    st = lambda n: jnp.stack(outs[n])
    return (hp.reshape(bp, seq, d), hs.reshape(bs, 1, d), st("kp"), st("vp"), st("gp"), st("cp"),
            st("ks"), st("vs"), st("gs"), st("cs"))
```

```python
import functools
import math

import jax
import jax.numpy as jnp
from jax import lax
from jax.experimental import pallas as pl
from jax.experimental.pallas import tpu as pltpu

F32 = jnp.float32
BF16 = jnp.bfloat16
NORM_EPS = 1e-6
LOG2E = 1.4426950408889634
LANES = 128
SUBLANES = 8
GDN_CHUNK = 128
INV_BLOCK = 16
SB_BLOCK = 256
SB_HEADS_PER_STEP = 4
PAGES_PER_STEP = 8
VMEM_LIMIT = 56 * 1024 * 1024


def _cparams(sem):
    return pltpu.CompilerParams(dimension_semantics=sem, vmem_limit_bytes=VMEM_LIMIT)


def _tile(n, pref):
    if n <= pref:
        return n
    t = pref
    while n % t:
        t //= 2
    return t


def _mm(a, b):
    return jnp.dot(a.astype(BF16), b.astype(BF16), preferred_element_type=F32)


def _mm_nt(a, b):
    return lax.dot_general(a.astype(BF16), b.astype(BF16), (((1,), (1,)), ((), ())), preferred_element_type=F32)


def _mm_tn(a, b):
    return lax.dot_general(a.astype(BF16), b.astype(BF16), (((0,), (0,)), ((), ())), preferred_element_type=F32)


def _split3(x):
    hi = x.astype(BF16)
    r = x - hi.astype(F32)
    mid = r.astype(BF16)
    lo = (r - mid.astype(F32)).astype(BF16)
    return hi, mid, lo


def _dot_01_rhs(x, m01):
    return sum(jnp.dot(p, m01, preferred_element_type=F32) for p in _split3(x))


def _dot_01_lhs(m01, x):
    return sum(jnp.dot(m01, p, preferred_element_type=F32) for p in _split3(x))


def _softplus(x):
    return jnp.maximum(x, 0.0) + jnp.log(1.0 + jnp.exp2(jnp.abs(x) * (-LOG2E)))


def _bmm(a, b):
    return lax.dot_general(a.astype(BF16), b.astype(BF16), (((2,), (1,)), ((0,), (0,))), preferred_element_type=F32)


def _bmm_nt(a, b):
    return lax.dot_general(a.astype(BF16), b.astype(BF16), (((2,), (2,)), ((0,), (0,))), preferred_element_type=F32)


def _bmm_tn(a, b):
    return lax.dot_general(a.astype(BF16), b.astype(BF16), (((1,), (1,)), ((0,), (0,))), preferred_element_type=F32)


def _silu(x):
    return x * jax.nn.sigmoid(x)


def _rms(x):
    return x * lax.rsqrt(jnp.mean(x * x, axis=-1, keepdims=True) + NORM_EPS)


def _prep_w_in_kernel(a_ref, b_ref, o_ref, oba_ref, *, n_direct, shift, n_small):
    j = pl.program_id(0)

    @pl.when(j < n_direct)
    def _():
        o_ref[...] = a_ref[...].astype(BF16)

    lane = lax.broadcasted_iota(jnp.int32, a_ref.shape, 1)

    @pl.when(j >= n_direct)
    def _():
        a = pltpu.roll(a_ref[...], LANES - shift, axis=1)
        b = pltpu.roll(b_ref[...], LANES - shift, axis=1)
        o_ref[...] = jnp.where(lane < LANES - shift, a, b).astype(BF16)

    @pl.when(j == n_direct)
    def _():
        oba_ref[...] = jnp.where(lane < n_small, a_ref[...], 0.0).astype(BF16)


def _prep_w_in(w, c_small, n_small):
    d, n = w.shape
    assert c_small % LANES == 0 and 0 < n_small < LANES and (n - n_small) % LANES == 0
    n_direct = c_small // LANES
    n_out = (n - n_small) // LANES
    kern = functools.partial(_prep_w_in_kernel, n_direct=n_direct, shift=n_small, n_small=n_small)
    return pl.pallas_call(
        kern,
        grid=(n_out,),
        in_specs=[
            pl.BlockSpec((d, LANES), lambda j: (0, j)),
            pl.BlockSpec((d, LANES), lambda j: (0, jnp.where(j < n_direct, 0, j + 1))),
        ],
        out_specs=[pl.BlockSpec((d, LANES), lambda j: (0, j)), pl.BlockSpec((d, LANES), lambda j: (0, 0))],
        out_shape=[jax.ShapeDtypeStruct((d, n - n_small), BF16), jax.ShapeDtypeStruct((d, LANES), BF16)],
        compiler_params=_cparams(("arbitrary",)),
        name="prep_w_in",
    )(w, w)


def _ada_kernel(cp_ref, cs_ref, w_ref, b_ref, op_ref, os_ref):
    w = w_ref[...].astype(BF16)
    b = b_ref[...]
    for c_ref, o_ref in ((cp_ref, op_ref), (cs_ref, os_ref)):
        o_ref[...] = jnp.dot(_silu(c_ref[...]).astype(BF16), w, preferred_element_type=F32) + b


def _ada(c_p, c_s, w, b):
    d, n = w.shape
    tn = _tile(n, 1024)
    bp, bs = c_p.shape[0], c_s.shape[0]
    return pl.pallas_call(
        _ada_kernel,
        grid=(n // tn,),
        in_specs=[
            pl.BlockSpec((bp, d), lambda j: (0, 0)),
            pl.BlockSpec((bs, d), lambda j: (0, 0)),
            pl.BlockSpec((d, tn), lambda j: (0, j)),
            pl.BlockSpec((1, tn), lambda j: (0, j)),
        ],
        out_specs=[pl.BlockSpec((bp, tn), lambda j: (0, j)), pl.BlockSpec((bs, tn), lambda j: (0, j))],
        out_shape=[jax.ShapeDtypeStruct((bp, n), F32), jax.ShapeDtypeStruct((bs, n), F32)],
        compiler_params=_cparams(("parallel",)),
        name="ada_mod",
    )(c_p, c_s, w, b.reshape(1, n))


def _row_tile(m, mod3, rows_per_group, pref):
    return m if mod3.shape[1] != 1 else _tile(rows_per_group, pref)


def _mod_spec(mod3, comp, d, tm, rows_per_group, tn=None):
    per_row = mod3.shape[1] != 1
    assert not per_row or mod3.shape[1] == tm
    if tn is None:
        col = lambda j: comp
        tn = d
    else:
        col = lambda j: comp * (d // tn) + j
    if per_row:
        return pl.BlockSpec((1, tm, tn), lambda i, j, *_: (0, 0, col(j)))
    return pl.BlockSpec((1, 1, tn), lambda i, j, *_: (i * tm // rows_per_group, 0, col(j)))


def _inproj_kernel(x_ref, nw_ref, sc_ref, sh_ref, w_ref, wba_ref, og_ref, oq_ref, ok_ref, ov_ref, oba_ref, *rest,
                   n_g, n_h, n_bat):
    obat_ref = rest[0] if n_bat else None
    h_scr = rest[-1]
    j = pl.program_id(1)

    @pl.when(j == 0)
    def _():
        h = _rms(x_ref[...]) * nw_ref[...]
        h = (h * (1.0 + sc_ref[0]) + sh_ref[0]).astype(BF16)
        h_scr[...] = h
        ba = jnp.dot(h, wba_ref[...], preferred_element_type=F32)
        oba_ref[...] = ba
        if n_bat:
            obat_ref[...] = ba.T[:n_bat]

    acc = jnp.dot(h_scr[...], w_ref[...], preferred_element_type=F32)
    for ref, lo, hi in ((og_ref, 0, n_g), (oq_ref, n_g, n_g + n_h), (ok_ref, n_g + n_h, n_g + 2 * n_h),
                        (ov_ref, n_g + 2 * n_h, n_g + 3 * n_h)):
        @pl.when((j >= lo) & (j < hi))
        def _(ref=ref):
            ref[...] = acc


def _inproj(x, nw, mod3, rows_per_group, w_main, w_ba, gdn_cols, sb_cols, n_bat):
    m, d = x.shape
    tm = _row_tile(m, mod3, rows_per_group, 1024)
    tn = _tile(math.gcd(gdn_cols, sb_cols), 512)
    n_g, n_h = gdn_cols // tn, sb_cols // tn
    nj = n_g + 3 * n_h

    def seg(lo, n):
        return lambda i, j: (i, jnp.clip(j - lo, 0, n - 1))

    kern = functools.partial(_inproj_kernel, n_g=n_g, n_h=n_h, n_bat=n_bat)
    return pl.pallas_call(
        kern,
        grid=(m // tm, nj),
        in_specs=[
            pl.BlockSpec((tm, d), lambda i, j: (i, 0)),
            pl.BlockSpec((1, d), lambda i, j: (0, 0)),
            _mod_spec(mod3, 1, d, tm, rows_per_group),
            _mod_spec(mod3, 0, d, tm, rows_per_group),
            pl.BlockSpec((d, tn), lambda i, j: (0, j)),
            pl.BlockSpec((d, LANES), lambda i, j: (0, 0)),
        ],
        out_specs=[
            pl.BlockSpec((tm, tn), seg(0, n_g)),
            pl.BlockSpec((tm, tn), seg(n_g, n_h)),
            pl.BlockSpec((tm, tn), seg(n_g + n_h, n_h)),
            pl.BlockSpec((tm, tn), seg(n_g + 2 * n_h, n_h)),
            pl.BlockSpec((tm, LANES), lambda i, j: (i, 0)),
        ] + ([pl.BlockSpec((n_bat, tm), lambda i, j: (0, i))] if n_bat else []),
        out_shape=[
            jax.ShapeDtypeStruct((m, gdn_cols), F32),
            jax.ShapeDtypeStruct((m, sb_cols), F32),
            jax.ShapeDtypeStruct((m, sb_cols), F32),
            jax.ShapeDtypeStruct((m, sb_cols), F32),
            jax.ShapeDtypeStruct((m, LANES), F32),
        ] + ([jax.ShapeDtypeStruct((n_bat, m), F32)] if n_bat else []),
        scratch_shapes=[pltpu.VMEM((tm, d), BF16)],
        compiler_params=_cparams(("parallel", "arbitrary")),
        name="in_proj",
    )(x, nw.reshape(1, d), mod3, mod3, w_main, w_ba)


def _unit_lower_inverse_minus_eye(a, ri, ci):
    c = a.shape[-1]
    blk = ((ri // INV_BLOCK) == (ci // INV_BLOCK))[None]
    ad = jnp.where(blk, a, 0.0)
    off = a - ad
    xn = -ad
    p = ad
    for _ in range(int(math.log2(INV_BLOCK)) - 1):
        p = _bmm(p, p)
        xn = xn + p + _bmm(xn, p)
    m = off + _bmm(xn, off)
    yn = -m
    p = m
    for _ in range(int(math.log2(c // INV_BLOCK)) - 1):
        p = _bmm(p, p)
        yn = yn + p + _bmm(yn, p)
    return yn + xn + _bmm(yn, xn)


def _gdn_chunk_kernel(g_ref, ba_ref, bat_ref, cb_ref, s0_ref, cw_ref, arow_ref, drow_ref, acol_ref, dcol_ref, nw_ref,
                      o_ref, s_ref, hist_ref, *, nh, hd, chunk, kw):
    c = pl.program_id(1)

    @pl.when(c == 0)
    def _():
        s_ref[...] = s0_ref[...]
        hist_ref[...] = cb_ref[...]

    gw = nh * hd
    ri = lax.broadcasted_iota(jnp.int32, (chunk, chunk), 0)
    ci = lax.broadcasted_iota(jnp.int32, (chunk, chunk), 1)
    incl = ci <= ri
    strict = ci < ri
    ltri = incl.astype(BF16)
    utri = (ri <= ci).astype(BF16)

    ba = ba_ref[...]
    bat = bat_ref[...]
    beta_all = jax.nn.sigmoid(ba)
    g_all = -jnp.exp(arow_ref[...]) * _softplus(ba + drow_ref[...])
    g_t = -jnp.exp(acol_ref[...]) * _softplus(bat + dcol_ref[...])
    gc_all = _dot_01_lhs(ltri, g_all)
    gc_t = _dot_01_rhs(g_t, utri)
    scale = hd ** -0.5

    hist = hist_ref[0]
    x_all = g_ref[...]
    cw = cw_ref[...]
    nw = nw_ref[...]
    pad = hist.shape[0]

    xw = jnp.concatenate([hist, x_all[:, :3 * gw]], axis=0)
    y = cw[kw - 1:kw] * xw[pad:]
    for i in range(kw - 1):
        sft = kw - 1 - i
        y = y + cw[i:i + 1] * xw[pad - sft:pad - sft + chunk]
    y = _silu(y)

    def heads(arr, base):
        return jnp.stack([arr[:, base + h * hd:base + (h + 1) * hd] for h in range(nh)])

    def l2n(t):
        return t * lax.rsqrt(jnp.sum(t * t, axis=-1, keepdims=True) + NORM_EPS)

    q = l2n(heads(y, 0)) * scale
    k = l2n(heads(y, gw))
    v = heads(y, 2 * gw)
    beta = jnp.stack([beta_all[:, h:h + 1] for h in range(nh)])
    gc = jnp.stack([gc_all[:, nh + h:nh + h + 1] for h in range(nh)])
    gc_row = jnp.stack([gc_t[nh + h:nh + h + 1, :] for h in range(nh)])
    gl = gc[:, chunk - 1:chunk, :]
    decay = jnp.exp(jnp.where(incl[None], gc - gc_row, -1e30))
    kb = k * beta
    a = jnp.where(strict[None], _bmm_nt(kb, k) * decay, 0.0)
    tn = _unit_lower_inverse_minus_eye(a, ri, ci)
    egc = jnp.exp(gc)
    r = jnp.concatenate([v * beta, kb * egc], axis=2)
    uw = r + _bmm(tn, r)
    u, w = uw[:, :, :hd], uw[:, :, hd:]
    qk = jnp.where(incl[None], _bmm_nt(q, k) * decay, 0.0)
    s = s_ref[0]
    v_new = u - _bmm(w, s)
    o = _bmm(q * egc, s) + _bmm(qk, v_new)
    s_ref[0] = s * jnp.exp(gl) + _bmm_tn(k * jnp.exp(gl - gc), v_new)
    out = (_rms(o) * nw * _silu(heads(x_all, 3 * gw))).astype(o_ref.dtype)
    for h in range(nh):
        o_ref[:, h * hd:(h + 1) * hd] = out[h]

    hist_ref[0] = x_all[chunk - pad:, :3 * gw]


def _gdn_chunked(gdn, ba, bat, conv_buf, state0, conv_w, a_log, dt_bias, norm_w, batch, seq):
    m, gcols = gdn.shape
    nh = a_log.shape[0]
    hd = norm_w.shape[0]
    gw = nh * hd
    kw = conv_w.shape[0]
    chunk = _tile(seq, GDN_CHUNK)
    nc = seq // chunk
    n_bat = bat.shape[0]
    pad = SUBLANES
    cb = jnp.zeros((batch, pad, 3 * gw), F32).at[:, pad - (kw - 1):].set(conv_buf.astype(F32))
    cw = jnp.zeros((pad, 3 * gw), F32).at[:kw].set(conv_w.astype(F32))
    arow = jnp.zeros((1, LANES), F32).at[0, nh:2 * nh].set(a_log)
    drow = jnp.zeros((1, LANES), F32).at[0, nh:2 * nh].set(dt_bias)
    acol = jnp.zeros((n_bat, 1), F32).at[nh:2 * nh, 0].set(a_log)
    dcol = jnp.zeros((n_bat, 1), F32).at[nh:2 * nh, 0].set(dt_bias)
    kern = functools.partial(_gdn_chunk_kernel, nh=nh, hd=hd, chunk=chunk, kw=kw)
    const2 = lambda b, c: (0, 0)
    o, s, hist = pl.pallas_call(
        kern,
        grid=(batch, nc),
        in_specs=[
            pl.BlockSpec((chunk, gcols), lambda b, c: (b * nc + c, 0)),
            pl.BlockSpec((chunk, LANES), lambda b, c: (b * nc + c, 0)),
            pl.BlockSpec((n_bat, chunk), lambda b, c: (0, b * nc + c)),
            pl.BlockSpec((1, pad, 3 * gw), lambda b, c: (b, 0, 0)),
            pl.BlockSpec((1, nh, hd, hd), lambda b, c: (b, 0, 0, 0)),
            pl.BlockSpec((pad, 3 * gw), const2),
            pl.BlockSpec((1, LANES), const2),
            pl.BlockSpec((1, LANES), const2),
            pl.BlockSpec((n_bat, 1), const2),
            pl.BlockSpec((n_bat, 1), const2),
            pl.BlockSpec((1, hd), const2),
        ],
        out_specs=[
            pl.BlockSpec((chunk, gw), lambda b, c: (b * nc + c, 0)),
            pl.BlockSpec((1, nh, hd, hd), lambda b, c: (b, 0, 0, 0)),
            pl.BlockSpec((1, pad, 3 * gw), lambda b, c: (b, 0, 0)),
        ],
        out_shape=[
            jax.ShapeDtypeStruct((m, gw), BF16),
            jax.ShapeDtypeStruct((batch, nh, hd, hd), F32),
            jax.ShapeDtypeStruct((batch, pad, 3 * gw), F32),
        ],
        compiler_params=_cparams(("parallel", "arbitrary")),
        name="gdn_chunked",
    )(gdn, ba, bat, cb, state0.astype(F32), cw, arow, drow, acol, dcol, norm_w.reshape(1, hd).astype(F32))
    return o, s, hist[:, pad - (kw - 1):]


def _gdn_step_kernel(g_ref, ba_ref, cb_ref, s0_ref, cw_ref, arow_ref, drow_ref, nw_ref, o_ref, s_ref, *, nh, hd, kw):
    gw = nh * hd
    x = g_ref[0]
    cb = cb_ref[0]
    cw = cw_ref[...]
    y = cw[kw - 1:kw] * x[:, :3 * gw]
    for i in range(kw - 1):
        y = y + cw[i:i + 1] * cb[i:i + 1]
    y = _silu(y)
    ba = ba_ref[0]
    beta_all = jax.nn.sigmoid(ba)
    g_all = -jnp.exp(arow_ref[...]) * _softplus(ba + drow_ref[...])
    nw = nw_ref[...]
    row = lax.broadcasted_iota(jnp.int32, (SUBLANES, hd), 0)
    scale = hd ** -0.5

    def l2n(t):
        return t * lax.rsqrt(jnp.sum(t * t, axis=-1, keepdims=True) + NORM_EPS)

    for h in range(nh):
        q = l2n(y[:, h * hd:(h + 1) * hd]) * scale
        k = l2n(y[:, gw + h * hd:gw + (h + 1) * hd])
        v = y[:, 2 * gw + h * hd:2 * gw + (h + 1) * hd]
        beta = beta_all[:, h:h + 1]
        eg = jnp.exp(g_all[:, nh + h:nh + h + 1])
        s = s0_ref[0, h]
        lhs = jnp.where(row == 0, jnp.broadcast_to(k, (SUBLANES, hd)),
                        jnp.where(row == 1, jnp.broadcast_to(q, (SUBLANES, hd)), 0.0))
        ks_qs = _mm(lhs, s)
        ks, qs = ks_qs[0:1], ks_qs[1:2]
        v_new = beta * v - (beta * eg) * ks
        qk = jnp.sum(q * k, axis=-1, keepdims=True)
        o = eg * qs + qk * v_new
        k8 = jnp.where(row == 0, jnp.broadcast_to(k, (SUBLANES, hd)), 0.0)
        v8 = jnp.where(row == 0, jnp.broadcast_to(v_new, (SUBLANES, hd)), 0.0)
        s_ref[0, h] = s * eg + _mm_tn(k8, v8)
        z = x[:, 3 * gw + h * hd:3 * gw + (h + 1) * hd]
        o_ref[0, :, h * hd:(h + 1) * hd] = (_rms(o) * nw * _silu(z)).astype(o_ref.dtype)


def _gdn_step(gdn, ba, conv_buf, state0, conv_w, a_log, dt_bias, norm_w):
    b, gcols = gdn.shape
    nh = a_log.shape[0]
    hd = norm_w.shape[0]
    gw = nh * hd
    kw = conv_w.shape[0]
    arow = jnp.zeros((1, LANES), F32).at[0, nh:2 * nh].set(a_log)
    drow = jnp.zeros((1, LANES), F32).at[0, nh:2 * nh].set(dt_bias)
    kern = functools.partial(_gdn_step_kernel, nh=nh, hd=hd, kw=kw)
    const2 = lambda i: (0, 0)
    o, s = pl.pallas_call(
        kern,
        grid=(b,),
        in_specs=[
            pl.BlockSpec((1, 1, gcols), lambda i: (i, 0, 0)),
            pl.BlockSpec((1, 1, LANES), lambda i: (i, 0, 0)),
            pl.BlockSpec((1, kw - 1, 3 * gw), lambda i: (i, 0, 0)),
            pl.BlockSpec((1, nh, hd, hd), lambda i: (i, 0, 0, 0)),
            pl.BlockSpec((kw, 3 * gw), const2),
            pl.BlockSpec((1, LANES), const2),
            pl.BlockSpec((1, LANES), const2),
            pl.BlockSpec((1, hd), const2),
        ],
        out_specs=[
            pl.BlockSpec((1, 1, gw), lambda i: (i, 0, 0)),
            pl.BlockSpec((1, nh, hd, hd), lambda i: (i, 0, 0, 0)),
        ],
        out_shape=[jax.ShapeDtypeStruct((b, 1, gw), BF16), jax.ShapeDtypeStruct((b, nh, hd, hd), F32)],
        compiler_params=_cparams(("parallel",)),
        name="gdn_step",
    )(gdn.reshape(b, 1, gcols), ba.reshape(b, 1, LANES), conv_buf.astype(F32), state0.astype(F32),
      conv_w.astype(F32), arow, drow, norm_w.reshape(1, hd).astype(F32))
    return o.reshape(b, gw), s


def _sb_block_update(q, k, v, bias, carry, acc, mask, utri):
    z, from_here, tot = _sb_scores(q, k, bias, mask, utri)
    return carry + tot, _sb_apply(z, from_here, carry, v, acc, mask)


def _sb_scores(q, k, bias, mask, utri):
    g, tq, _ = q.shape
    tk = k.shape[1]
    zb = _bmm_nt(q, k)
    z = jnp.stack([zb[i] + bias[i] for i in range(g)])
    nlk = _softplus(z)
    if mask is not None:
        nlk = jnp.where(mask[None], nlk, 0.0)
    hi = nlk.astype(BF16)
    lo = (nlk - hi.astype(F32)).astype(BF16)
    hilo = jnp.concatenate([hi, lo], axis=2).reshape(g * tq, 2 * tk)
    from_here = jnp.dot(hilo, utri, preferred_element_type=F32).reshape(g, tq, tk)
    return z, from_here, jnp.sum(nlk, axis=-1, keepdims=True)


def _sb_apply(z, from_here, carry, v, acc, mask):
    w = jnp.exp(z - from_here - carry)
    if mask is not None:
        w = jnp.where(mask[None], w, 0.0)
    return acc + _bmm(w, v)


def _sb_prompt_kernel(bias_ref, q_ref, k_ref, v_ref, o_ref, *, blk, hd, hp):
    qi = pl.program_id(2)
    hg = pl.program_id(1)
    bias = [bias_ref[hg * hp + i] for i in range(hp)]
    scale = hd ** -0.5
    q = jnp.stack([(q_ref[:, i * hd:(i + 1) * hd] * scale).astype(BF16) for i in range(hp)])
    ri = lax.broadcasted_iota(jnp.int32, (blk, blk), 0)
    ci = lax.broadcasted_iota(jnp.int32, (blk, blk), 1)
    utri = (ri >= ci).astype(BF16)
    utri = jnp.concatenate([utri, utri], axis=0)

    def heads(ref, start):
        return jnp.stack([ref[pl.ds(start, blk), i * hd:(i + 1) * hd] for i in range(hp)])

    start = pl.multiple_of(qi * blk, blk)
    carry, acc = _sb_block_update(q, heads(k_ref, start), heads(v_ref, start), bias,
                                  jnp.zeros((hp, blk, 1), F32), jnp.zeros((hp, blk, hd), F32), ci < ri, utri)

    def pair(it, ca):
        carry, acc = ca
        sa = pl.multiple_of((qi - 1 - 2 * it) * blk, blk)
        sb = pl.multiple_of((qi - 2 - 2 * it) * blk, blk)
        za, fa, ta = _sb_scores(q, heads(k_ref, sa), bias, None, utri)
        zb, fb, tb = _sb_scores(q, heads(k_ref, sb), bias, None, utri)
        acc = _sb_apply(za, fa, carry, heads(v_ref, sa), acc, None)
        acc = _sb_apply(zb, fb, carry + ta, heads(v_ref, sb), acc, None)
        return carry + ta + tb, acc

    def single(it, ca):
        return _sb_block_update(q, heads(k_ref, 0), heads(v_ref, 0), bias, ca[0], ca[1], None, utri)

    carry, acc = lax.fori_loop(0, lax.div(qi, 2), pair, (carry, acc))
    carry, acc = lax.fori_loop(0, lax.rem(qi, 2), single, (carry, acc))
    for i in range(hp):
        o_ref[:, i * hd:(i + 1) * hd] = acc[i].astype(o_ref.dtype)


def _sb_prompt(q, k, v, bias, batch, seq, hd):
    m, width = q.shape
    nh = width // hd
    hp = _tile(nh, SB_HEADS_PER_STEP)
    blk = _tile(seq, SB_BLOCK)
    nq = seq // blk
    kern = functools.partial(_sb_prompt_kernel, blk=blk, hd=hd, hp=hp)
    return pl.pallas_call(
        kern,
        grid=(batch, nh // hp, nq),
        in_specs=[
            pl.BlockSpec(memory_space=pltpu.SMEM),
            pl.BlockSpec((blk, hp * hd), lambda b, h, i: (b * nq + i, h)),
            pl.BlockSpec((seq, hp * hd), lambda b, h, i: (b, h)),
            pl.BlockSpec((seq, hp * hd), lambda b, h, i: (b, h)),
        ],
        out_specs=pl.BlockSpec((blk, hp * hd), lambda b, h, i: (b * nq + i, h)),
        out_shape=jax.ShapeDtypeStruct((m, width), BF16),
        compiler_params=_cparams(("parallel", "parallel", "arbitrary")),
        name="sb_prompt",
    )(bias.astype(F32), q, k, v)


def _paged_self_term(q, k_new, v_new, bcol, page, n_pages):
    nh, hd = q.shape
    z = jnp.sum(q * k_new, axis=-1, keepdims=True) * (hd ** -0.5) + bcol
    pos = jnp.full((nh, 1), n_pages * page, jnp.int32)
    visible = pos < pos
    nlk = jnp.where(visible, _softplus(z), 0.0)
    w = jnp.where(visible, jnp.exp(z - nlk), 0.0)
    return w * v_new, jnp.broadcast_to(nlk, (nh, LANES))


def _paged_pages(q, brow, k_pages, v_pages, run, acc):
    nh, hd = q.shape
    width = brow.shape[1]
    nt = width // LANES
    sub = lax.broadcasted_iota(jnp.int32, (nh, width), 0)
    lane = lax.broadcasted_iota(jnp.int32, (nh, width), 1)
    own = (lane % nh) == sub
    ri = lax.broadcasted_iota(jnp.int32, (LANES, 2 * LANES), 0)
    ci = lax.broadcasted_iota(jnp.int32, (LANES, 2 * LANES), 1)
    usum = ((ri > ci) | (ci >= LANES)).astype(BF16)
    qs = (q * (hd ** -0.5)).astype(BF16)
    z = [_mm_nt(qs, kp) + brow for kp in k_pages]
    nlk = [jnp.where(own, _softplus(zp), 0.0) for zp in z]
    tiles = jnp.concatenate([n[:, t * LANES:(t + 1) * LANES] for n in nlk for t in range(nt)], axis=0)
    hi = tiles.astype(BF16)
    lo = (tiles - hi.astype(F32)).astype(BF16)
    su = jnp.dot(hi, usum, preferred_element_type=F32) + jnp.dot(lo, usum, preferred_element_type=F32)
    for p, vp in enumerate(v_pages):
        later = []
        for t in reversed(range(nt)):
            r0 = (p * nt + t) * nh
            later.append(su[r0:r0 + nh, :LANES] + run)
            run = run + su[r0:r0 + nh, LANES:]
        later = jnp.concatenate(later[::-1], axis=1)
        w = jnp.where(own, jnp.exp(z[p] - nlk[p] - later), 0.0)
        acc = acc + _mm(w, vp)
    return run, acc


def _sb_paged_kernel(pt_ref, q_ref, knew_ref, vnew_ref, brow_ref, bcol_ref, *refs, page, npp, n_pages):
    k_refs, v_refs = refs[:npp], refs[npp:2 * npp]
    o_ref, carry_ref = refs[2 * npp], refs[2 * npp + 1]

    @pl.when(pl.program_id(1) == 0)
    def _():
        o_ref[0], carry_ref[...] = _paged_self_term(q_ref[0], knew_ref[0], vnew_ref[0], bcol_ref[...], page, n_pages)

    carry_ref[...], o_ref[0] = _paged_pages(q_ref[0], brow_ref[...], [r[...] for r in k_refs],
                                             [r[...] for r in v_refs], carry_ref[...], o_ref[0])


def _sb_paged(q, k_new, v_new, bias, cache_k, cache_v, layer, page_table):
    b, nh, hd = q.shape
    n_pool, page = cache_k.shape[1], cache_k.shape[2]
    n_pages = page_table.shape[1]
    npp = _tile(n_pages, PAGES_PER_STEP)
    width = page * nh
    ck = cache_k.reshape(cache_k.shape[0], n_pool, width, hd)
    cv = cache_v.reshape(cache_v.shape[0], n_pool, width, hd)
    brow = jnp.tile(bias.astype(F32), page).reshape(1, width)

    def page_spec(p):
        return pl.BlockSpec((None, None, width, hd),
                            lambda i, s, pt: (layer, pt[i * n_pages + n_pages - 1 - (s * npp + p)], 0, 0))

    kern = functools.partial(_sb_paged_kernel, page=page, npp=npp, n_pages=n_pages)
    head_spec = pl.BlockSpec((1, nh, hd), lambda i, s, pt: (i, 0, 0))
    grid_spec = pltpu.PrefetchScalarGridSpec(
        num_scalar_prefetch=1,
        grid=(b, n_pages // npp),
        in_specs=[head_spec, head_spec, head_spec, pl.BlockSpec((1, width), lambda i, s, pt: (0, 0)),
                  pl.BlockSpec((nh, 1), lambda i, s, pt: (0, 0))]
        + [page_spec(p) for p in range(npp)] + [page_spec(p) for p in range(npp)],
        out_specs=head_spec,
        scratch_shapes=[pltpu.VMEM((nh, LANES), F32)],
    )
    return pl.pallas_call(
        kern,
        grid_spec=grid_spec,
        out_shape=jax.ShapeDtypeStruct((b, nh, hd), F32),
        compiler_params=_cparams(("parallel", "arbitrary")),
        name="sb_paged",
    )(page_table.reshape(-1).astype(jnp.int32), q, k_new, v_new, brow, bias.astype(F32).reshape(nh, 1),
      *([ck] * npp), *([cv] * npp))


def _outproj_kernel(x_ref, ga_ref, a_ref, b_ref, wa_ref, wb_ref, o_ref):
    y = jnp.dot(a_ref[...], wa_ref[...], preferred_element_type=F32)
    y = y + jnp.dot(b_ref[...], wb_ref[...], preferred_element_type=F32)
    o_ref[...] = x_ref[...] + ga_ref[0] * y


def _outproj(x, mod3, rows_per_group, mix_a, mix_b, w_a, w_b):
    m, d = x.shape
    ka, kb = mix_a.shape[1], mix_b.shape[1]
    tm = _row_tile(m, mod3, rows_per_group, 512)
    tn = d
    return pl.pallas_call(
        _outproj_kernel,
        grid=(m // tm, d // tn),
        in_specs=[
            pl.BlockSpec((tm, tn), lambda i, j: (i, j)),
            _mod_spec(mod3, 2, d, tm, rows_per_group, tn),
            pl.BlockSpec((tm, ka), lambda i, j: (i, 0)),
            pl.BlockSpec((tm, kb), lambda i, j: (i, 0)),
            pl.BlockSpec((ka, tn), lambda i, j: (0, j)),
            pl.BlockSpec((kb, tn), lambda i, j: (0, j)),
        ],
        out_specs=pl.BlockSpec((tm, tn), lambda i, j: (i, j)),
        out_shape=jax.ShapeDtypeStruct((m, d), F32),
        compiler_params=_cparams(("parallel", "arbitrary")),
        name="out_proj",
    )(x, mod3, mix_a, mix_b, w_a, w_b)


def _page_copies(pt_ref, ck_ref, cv_ref, kbuf, vbuf, sem, step, slot, *, layer, ppg, n_pages):
    spb = n_pages // ppg
    b = lax.div(step, spb)
    s = lax.rem(step, spb)
    copies = []
    for p in range(ppg):
        phys = pt_ref[b * n_pages + (n_pages - 1 - (s * ppg + p))]
        copies.append(pltpu.make_async_copy(ck_ref.at[layer, phys], kbuf.at[slot, p], sem.at[slot]))
        copies.append(pltpu.make_async_copy(cv_ref.at[layer, phys], vbuf.at[slot, p], sem.at[slot]))
    return copies


def _ffn_kernel(*refs, final_norm, paged):
    if paged is None:
        x_ref, nw_ref, sh_ref, sc_ref, ga_ref, fw_ref, wg_ref, wu_ref, wd_ref, o_ref, h_scr, acc_scr = refs
    else:
        (pt_ref, x_ref, nw_ref, sh_ref, sc_ref, ga_ref, fw_ref, wg_ref, wu_ref, wd_ref, q_ref, knew_ref, vnew_ref,
         brow_ref, bcol_ref, ck_ref, cv_ref, o_ref, osb_ref, h_scr, acc_scr, kbuf, vbuf, carry_scr, sem) = refs
        page, ppg, n_pages = paged["page"], paged["ppg"], paged["n_pages"]
        spb = n_pages // ppg
        n_steps = paged["nb"] * spb
        copies = functools.partial(_page_copies, pt_ref, ck_ref, cv_ref, kbuf, vbuf, sem, layer=paged["layer"],
                                   ppg=ppg, n_pages=n_pages)
        t = pl.program_id(0) * pl.num_programs(1) + pl.program_id(1)

        @pl.when(t == 0)
        def _():
            for c in copies(0, 0):
                c.start()

        @pl.when(t + 1 < n_steps)
        def _():
            for c in copies(t + 1, lax.rem(t + 1, 2)):
                c.start()

    f = pl.program_id(1)

    @pl.when(f == 0)
    def _():
        h = _rms(x_ref[...]) * nw_ref[...]
        h_scr[...] = (h * (1.0 + sc_ref[0]) + sh_ref[0]).astype(BF16)
        acc_scr[...] = jnp.zeros_like(acc_scr)

    h = h_scr[...]
    g = jnp.dot(h, wg_ref[...], preferred_element_type=F32)
    u = jnp.dot(h, wu_ref[...], preferred_element_type=F32)
    acc_scr[...] += jnp.dot((_silu(g) * u).astype(BF16), wd_ref[...], preferred_element_type=F32)

    @pl.when(f == pl.num_programs(1) - 1)
    def _():
        y = x_ref[...] + ga_ref[0] * acc_scr[...]
        if final_norm:
            y = _rms(y) * fw_ref[...]
        o_ref[...] = y

    if paged is not None:
        @pl.when(t < n_steps)
        def _():
            slot = lax.rem(t, 2)
            for c in copies(t, slot):
                c.wait()
            b = lax.div(t, spb)

            @pl.when(lax.rem(t, spb) == 0)
            def _():
                osb_ref[b], carry_scr[...] = _paged_self_term(q_ref[b], knew_ref[b], vnew_ref[b], bcol_ref[...],
                                                              page, n_pages)

            carry_scr[...], osb_ref[b] = _paged_pages(
                q_ref[b], brow_ref[...], [kbuf[slot, p] for p in range(ppg)], [vbuf[slot, p] for p in range(ppg)],
                carry_scr[...], osb_ref[b])


def _ffn_tiles(m, mod3, rows_per_group, ff, hosts_pages):
    return _row_tile(m, mod3, rows_per_group, 512), _tile(ff, 256 if hosts_pages else 512)


def _ffn(x, nw, mod3, rows_per_group, fw, w_gate, w_up, w_down, final_norm, paged=None):
    m, d = x.shape
    ff = w_gate.shape[1]
    tm, tf = _ffn_tiles(m, mod3, rows_per_group, ff, paged is not None)
    specs = [
        pl.BlockSpec((tm, d), lambda i, f, *_: (i, 0)),
        pl.BlockSpec((1, d), lambda i, f, *_: (0, 0)),
        _mod_spec(mod3, 3, d, tm, rows_per_group),
        _mod_spec(mod3, 4, d, tm, rows_per_group),
        _mod_spec(mod3, 5, d, tm, rows_per_group),
        pl.BlockSpec((1, d), lambda i, f, *_: (0, 0)),
        pl.BlockSpec((d, tf), lambda i, f, *_: (0, f)),
        pl.BlockSpec((d, tf), lambda i, f, *_: (0, f)),
        pl.BlockSpec((tf, d), lambda i, f, *_: (f, 0)),
    ]
    args = [x, nw.reshape(1, d), mod3, mod3, mod3, fw.reshape(1, d), w_gate, w_up, w_down]
    out_spec = pl.BlockSpec((tm, d), lambda i, f, *_: (i, 0))
    out_shape = jax.ShapeDtypeStruct((m, d), F32)
    scratch = [pltpu.VMEM((tm, d), BF16), pltpu.VMEM((tm, d), F32)]
    grid = (m // tm, ff // tf)
    if paged is None:
        return pl.pallas_call(
            functools.partial(_ffn_kernel, final_norm=final_norm, paged=None),
            grid=grid, in_specs=specs, out_specs=out_spec, out_shape=out_shape, scratch_shapes=scratch,
            compiler_params=_cparams(("parallel", "arbitrary")), name="ffn",
        )(*args)

    q, k_new, v_new, bias, cache_k, cache_v, layer, page_table = paged
    nb, nh, hd = q.shape
    n_pool, page = cache_k.shape[1], cache_k.shape[2]
    n_pages = page_table.shape[1]
    ppg = _tile(n_pages, PAGES_PER_STEP)
    width = page * nh
    assert nb * (n_pages // ppg) <= grid[0] * grid[1]
    ck = cache_k.reshape(cache_k.shape[0], n_pool, width, hd)
    cv = cache_v.reshape(cache_v.shape[0], n_pool, width, hd)
    brow = jnp.tile(bias.astype(F32), page).reshape(1, width)
    whole = lambda shape: pl.BlockSpec(shape, lambda i, f, *_: (0,) * len(shape))
    info = dict(layer=layer, page=page, ppg=ppg, n_pages=n_pages, nb=nb)
    grid_spec = pltpu.PrefetchScalarGridSpec(
        num_scalar_prefetch=1,
        grid=grid,
        in_specs=specs + [whole((nb, nh, hd))] * 3 + [whole((1, width)), whole((nh, 1)),
                                                       pl.BlockSpec(memory_space=pl.ANY),
                                                       pl.BlockSpec(memory_space=pl.ANY)],
        out_specs=[out_spec, whole((nb, nh, hd))],
        scratch_shapes=scratch + [pltpu.VMEM((2, ppg, width, hd), F32), pltpu.VMEM((2, ppg, width, hd), F32),
                                  pltpu.VMEM((nh, LANES), F32), pltpu.SemaphoreType.DMA((2,))],
    )
    return pl.pallas_call(
        functools.partial(_ffn_kernel, final_norm=final_norm, paged=info),
        grid_spec=grid_spec,
        out_shape=[out_shape, jax.ShapeDtypeStruct((nb, nh, hd), F32)],
        compiler_params=_cparams(("arbitrary", "arbitrary")),
        name="ffn_paged",
    )(page_table.reshape(-1).astype(jnp.int32), *args, q, k_new, v_new, brow, bias.astype(F32).reshape(nh, 1), ck, cv)


def kernel(x_prompt, x_sample, c_prompt, c_sample, cache_k, cache_v, state_gdn, state_conv, page_table, norm1_w, norm2_w, w_ada, b_ada, w_in, conv_w, a_log, dt_bias, gdn_norm_w, sb_bias, w_out, w_gate, w_up, w_down, final_norm_w):
    depth = w_in.shape[0]
    bp, seq, d = x_prompt.shape
    bs, seq_s, _ = x_sample.shape
    assert seq_s == 1, "the sample group advances one token per step"
    nh_g, nh_s = a_log.shape[1], sb_bias.shape[1]
    hd = gdn_norm_w.shape[1]
    gw, sw = nh_g * hd, nh_s * hd
    kw = conv_w.shape[1]
    c_ba = 4 * gw
    c_sb = c_ba + 2 * nh_g
    n_bat = -(-2 * nh_g // SUBLANES) * SUBLANES

    hp = x_prompt.reshape(bp * seq, d)
    hs = x_sample.reshape(bs, d)
    outs = {n: [] for n in ("kp", "vp", "gp", "cp", "ks", "vs", "gs", "cs")}
    for l in range(depth):
        w_main, w_ba = _prep_w_in(w_in[l], c_ba, c_sb - c_ba)
        wo_a, wo_b = w_out[l][:gw].astype(BF16), w_out[l][gw:].astype(BF16)
        wg, wu, wd = w_gate[l].astype(BF16), w_up[l].astype(BF16), w_down[l].astype(BF16)
        last = l == depth - 1

        mod_p, mod_s = _ada(c_prompt, c_sample, w_ada[l], b_ada[l])
        mod_p3 = mod_p.reshape(bp, 1, 6 * d)
        mod_s3 = mod_s.reshape(1, bs, 6 * d)

        g_s, q_s, k_s, v_s, ba_s = _inproj(hs, norm1_w[l], mod_s3, 1, w_main, w_ba, 4 * gw, sw, 0)
        paged = (q_s.reshape(bs, nh_s, hd), k_s.reshape(bs, nh_s, hd), v_s.reshape(bs, nh_s, hd), sb_bias[l],
                 cache_k, cache_v, l, page_table)
        tm_p, tf_p = _ffn_tiles(bp * seq, mod_p3, seq, wg.shape[1], True)
        n_pages = page_table.shape[1]
        hosted = bs * (n_pages // _tile(n_pages, PAGES_PER_STEP)) <= (bp * seq // tm_p) * (wg.shape[1] // tf_p)

        g_p, q_p, k_p, v_p, ba_p, bat_p = _inproj(hp, norm1_w[l], mod_p3, seq, w_main, w_ba, 4 * gw, sw, n_bat)
        gdn_o, gdn_s, conv_o = _gdn_chunked(
            g_p, ba_p, bat_p, jnp.zeros((bp, kw - 1, 3 * gw), F32), jnp.zeros((bp, nh_g, hd, hd), F32),
            conv_w[l], a_log[l], dt_bias[l], gdn_norm_w[l], bp, seq)
        sb_o = _sb_prompt(q_p, k_p, v_p, sb_bias[l], bp, seq, hd)
        hp = _outproj(hp, mod_p3, seq, gdn_o, sb_o, wo_a, wo_b)
        if hosted:
            hp, sb_s = _ffn(hp, norm2_w[l], mod_p3, seq, final_norm_w, wg, wu, wd, last, paged)
        else:
            hp = _ffn(hp, norm2_w[l], mod_p3, seq, final_norm_w, wg, wu, wd, last)
            sb_s = _sb_paged(*paged)
        outs["kp"].append(k_p.reshape(bp, seq, nh_s, hd))
        outs["vp"].append(v_p.reshape(bp, seq, nh_s, hd))
        outs["gp"].append(gdn_s)
        outs["cp"].append(conv_o)

        gdn_o, gdn_s = _gdn_step(g_s, ba_s, state_conv[l], state_gdn[l], conv_w[l], a_log[l], dt_bias[l],
                                 gdn_norm_w[l])
        hs = _outproj(hs, mod_s3, 1, gdn_o, sb_s.reshape(bs, sw).astype(BF16), wo_a, wo_b)
        hs = _ffn(hs, norm2_w[l], mod_s3, 1, final_norm_w, wg, wu, wd, last)
        outs["ks"].append(k_s.reshape(bs, 1, nh_s, hd))
        outs["vs"].append(v_s.reshape(bs, 1, nh_s, hd))
        outs["gs"].append(gdn_s.astype(state_gdn.dtype))
        outs["cs"].append(jnp.concatenate([state_conv[l][:, 1:], g_s[:, None, :3 * gw]], axis=1).astype(state_conv.dtype))

    st = lambda n: jnp.stack(outs[n])
    return (hp.reshape(bp, seq, d), hs.reshape(bs, 1, d), st("kp"), st("vp"), st("gp"), st("cp"),
            st("ks"), st("vs"), st("gs"), st("cs"))
```
